```python
import math
import jax, jax.numpy as jnp
from jax import lax
import numpy as np

D_MODEL = 4096
BATCH = 32
SEQ = 256
DEPTH = 4
DEC_BATCH = 4
DEC_SEQ = 4096
PAST_LEN = 256

GRID_W = 64
N_EVEN = (DEPTH + 1) // 2
N_ODD = DEPTH // 2

NA_HEADS = 16
NA_HEAD_DIM = 128
NA_WIDTH = NA_HEADS * NA_HEAD_DIM
NA_ROWS = 8
NA_COLS = 16
Q_BLOCK = 128

SSD_HEAD_DIM = 64
SSD_INNER = D_MODEL // 2
SSD_HEADS = SSD_INNER // SSD_HEAD_DIM
SSD_GROUPS = 4
SSD_HPG = SSD_HEADS // SSD_GROUPS
SSD_STATE = 128
SSD_CONV_W = 5
SSD_CHUNK = 128
SSD_CONV_DIM = SSD_INNER + 2 * SSD_GROUPS * SSD_STATE

EVEN_SPLITS = (NA_WIDTH, 2 * NA_WIDTH, 3 * NA_WIDTH, 3 * NA_WIDTH + SSD_INNER,
               3 * NA_WIDTH + SSD_INNER + SSD_CONV_DIM)
EVEN_IN = 3 * NA_WIDTH + SSD_INNER + SSD_CONV_DIM + 2 * SSD_HEADS
EVEN_MIX = NA_WIDTH + SSD_INNER

SC_WIDTH = D_MODEL
SC_CONV_W = 3

D_FF = -(-8 * D_MODEL // (3 * 256)) * 256

EPS = 1e-6

kernel_name = "hybrid_natten_ssd_shortconv_dit_step"


def rms_norm(x, w):
    xf = x.astype(jnp.float32)
    y = xf * lax.rsqrt(jnp.mean(xf * xf, axis=-1, keepdims=True) + EPS)
    return (y * w.astype(jnp.float32)).astype(x.dtype)


def adaln(cond, w, b):
    m = jax.nn.silu(cond) @ w + b
    return jnp.split(m[:, None, :], 6, axis=-1)


def modulate(x, g, shift, scale):
    return rms_norm(x, g) * (1 + scale) + shift


def centred_dwconv(x, w, b=None):
    k_w = w.shape[0]
    pad = k_w // 2
    L = x.shape[1]
    xp = jnp.pad(x, ((0, 0), (pad, pad), (0, 0)))
    out = xp[:, 0:L] * w[0]
    for k in range(1, k_w):
        out = out + xp[:, k:k + L] * w[k]
    return out if b is None else out + b


def dense_attention(q, k, v):
    n, L, H, Dh = q.shape
    nb = L // Q_BLOCK
    scale = 1.0 / math.sqrt(Dh)
    qb = jnp.moveaxis(q.reshape(n, nb, Q_BLOCK, H, Dh), 1, 0)

    def block(qi):
        s = jnp.einsum('bqhd,bkhd->bhqk', qi, k).astype(jnp.float32) * scale
        p = jax.nn.softmax(s, axis=-1).astype(v.dtype)
        return jnp.einsum('bhqk,bkhd->bqhd', p, v)

    o = lax.map(block, qb)
    return jnp.moveaxis(o, 0, 1).reshape(n, L, H, Dh)


def neighbourhood_attention(q, k, v, ck, cv, rpb):
    n, L, H, Dh = q.shape
    rows = L // GRID_W
    kh = min(NA_ROWS, rows)
    kw = NA_COLS
    scale = 1.0 / math.sqrt(Dh)
    qg = q.reshape(n, rows, GRID_W, H, Dh)
    kg = k.reshape(n, rows, GRID_W, H, Dh)
    vg = v.reshape(n, rows, GRID_W, H, Dh)
    col = jnp.arange(GRID_W)
    col_start = jnp.clip(col - kw // 2, 0, GRID_W - kw)
    col_idx = col_start[:, None] + jnp.arange(kw)[None, :]
    dc = col_idx - col[:, None] + (NA_COLS - 1)

    def row_block(r):
        rs = jnp.clip(r - kh // 2, 0, rows - kh)
        k_rows = lax.dynamic_slice_in_dim(kg, rs, kh, axis=1)
        v_rows = lax.dynamic_slice_in_dim(vg, rs, kh, axis=1)
        k_win = k_rows[:, :, col_idx]
        v_win = v_rows[:, :, col_idx]
        q_r = lax.dynamic_index_in_dim(qg, r, axis=1, keepdims=False)
        dr = rs + jnp.arange(kh) - r + (NA_ROWS - 1)
        bias = rpb[:, dr[None, :, None], dc[:, None, :]]
        s_loc = jnp.einsum('bqhd,biqjhd->bhqij', q_r, k_win) * scale + bias[None]
        s_ctx = jnp.einsum('bqhd,bkhd->bhqk', q_r, ck) * scale
        s = jnp.concatenate([s_loc.reshape(n, H, GRID_W, kh * kw), s_ctx], axis=-1)
        p = jax.nn.softmax(s.astype(jnp.float32), axis=-1).astype(v.dtype)
        p_loc = p[..., :kh * kw].reshape(n, H, GRID_W, kh, kw)
        p_ctx = p[..., kh * kw:]
        return (jnp.einsum('bhqij,biqjhd->bqhd', p_loc, v_win)
                + jnp.einsum('bhqk,bkhd->bqhd', p_ctx, cv))

    o = lax.map(row_block, jnp.arange(rows))
    return jnp.moveaxis(o, 0, 1).reshape(n, L, H, Dh)


def ssd_chunk_scan(x, dt, A, bm, cm, h0):
    n, L = x.shape[:2]
    nc = L // SSD_CHUNK
    dtype = x.dtype
    xc = x.reshape(n, nc, SSD_CHUNK, SSD_GROUPS, SSD_HPG, SSD_HEAD_DIM)
    dtc = dt.reshape(n, nc, SSD_CHUNK, SSD_GROUPS, SSD_HPG)
    bc = bm.reshape(n, nc, SSD_CHUNK, SSD_GROUPS, SSD_STATE)
    cc = cm.reshape(n, nc, SSD_CHUNK, SSD_GROUPS, SSD_STATE)
    a_cum = jnp.cumsum(dtc * A, axis=2).transpose(0, 1, 3, 4, 2)
    seg = a_cum[..., :, None] - a_cum[..., None, :]
    lower = jnp.tril(jnp.ones((SSD_CHUNK, SSD_CHUNK), dtype=bool))
    decay_in = jnp.exp(jnp.where(lower, seg, -jnp.inf)).astype(dtype)
    xdt = xc * dtc[..., None].astype(dtype)
    cb = jnp.einsum('bclgn,bcsgn->bcgls', cc, bc)
    y_diag = jnp.einsum('bcgls,bcgrls,bcsgrp->bclgrp', cb, decay_in, xdt)
    decay_end = jnp.exp(a_cum[..., -1:] - a_cum).astype(dtype)
    states = jnp.einsum('bcsgn,bcgrs,bcsgrp->bcgrpn', bc, decay_end, xdt)
    chunk_decay = jnp.exp(a_cum[..., -1]).astype(dtype)

    def step(h, inp):
        st, dec = inp
        return (h * dec[..., None, None] + st).astype(h.dtype), h

    h_last, h_prev = lax.scan(step, h0, (jnp.moveaxis(states, 1, 0), jnp.moveaxis(chunk_decay, 1, 0)))
    h_prev = jnp.moveaxis(h_prev, 0, 1)
    y_off = jnp.einsum('bclgn,bcgrl,bcgrpn->bclgrp', cc, jnp.exp(a_cum).astype(dtype), h_prev)
    return (y_diag + y_off).reshape(n, L, SSD_GROUPS, SSD_HPG, SSD_HEAD_DIM), h_last


def ssd_mixer(z, xbc, dt_raw, conv_w, conv_b, a_log, dt_bias, d_skip, norm_w, h0):
    n, L, _ = xbc.shape
    xbc = jax.nn.silu(centred_dwconv(xbc, conv_w, conv_b))
    xs, bm, cm = jnp.split(xbc, [SSD_INNER, SSD_INNER + SSD_GROUPS * SSD_STATE], axis=-1)
    xs = xs.reshape(n, L, SSD_GROUPS, SSD_HPG, SSD_HEAD_DIM)
    bm = bm.reshape(n, L, SSD_GROUPS, SSD_STATE)
    cm = cm.reshape(n, L, SSD_GROUPS, SSD_STATE)
    dt = jax.nn.softplus(dt_raw.astype(jnp.float32).reshape(n, L, 2, SSD_HEADS)
                         + dt_bias.astype(jnp.float32))
    dt = dt.reshape(n, L, 2, SSD_GROUPS, SSD_HPG)
    A = -jnp.exp(a_log.astype(jnp.float32)).reshape(2, SSD_GROUPS, SSD_HPG)
    h0 = h0.reshape(n, 2, SSD_GROUPS, SSD_HPG, SSD_HEAD_DIM, SSD_STATE)
    y_f, h_f = ssd_chunk_scan(xs, dt[:, :, 0], A[0], bm, cm, h0[:, 0])
    y_b, h_b = ssd_chunk_scan(xs[:, ::-1], dt[:, ::-1, 1], A[1], bm[:, ::-1], cm[:, ::-1], h0[:, 1])
    y = y_f + y_b[:, ::-1] + xs * d_skip.reshape(SSD_GROUPS, SSD_HPG)[..., None]
    g = (y.reshape(n, L, SSD_INNER) * jax.nn.silu(z)).astype(jnp.float32)
    g = g.reshape(n, L, SSD_GROUPS, SSD_INNER // SSD_GROUPS)
    g = g * lax.rsqrt(jnp.mean(g * g, axis=-1, keepdims=True) + EPS)
    y_out = (g.reshape(n, L, SSD_INNER) * norm_w.astype(jnp.float32)).astype(z.dtype)
    h_fin = jnp.stack([h_f, h_b], axis=1).reshape(n, 2, SSD_HEADS, SSD_HEAD_DIM, SSD_STATE)
    return y_out, h_fin


def even_mixer(h, w_in, w_out, q_norm_w, k_norm_w, rpb, conv_w, conv_b, a_log, dt_bias, d_skip,
               ssd_norm_w, ctx_k=None, ctx_v=None, ssm_h0=None):
    n, L, _ = h.shape
    q, k, v, z, xbc, dt_raw = jnp.split(h @ w_in, EVEN_SPLITS, axis=-1)
    q = rms_norm(q.reshape(n, L, NA_HEADS, NA_HEAD_DIM), q_norm_w)
    k = rms_norm(k.reshape(n, L, NA_HEADS, NA_HEAD_DIM), k_norm_w)
    v = v.reshape(n, L, NA_HEADS, NA_HEAD_DIM)
    if ctx_k is None:
        attn = dense_attention(q, k, v)
        h0 = jnp.zeros((n, 2, SSD_HEADS, SSD_HEAD_DIM, SSD_STATE), dtype=h.dtype)
    else:
        attn = neighbourhood_attention(q, k, v, ctx_k, ctx_v, rpb)
        h0 = ssm_h0
    y_ssd, h_fin = ssd_mixer(z, xbc, dt_raw, conv_w, conv_b, a_log, dt_bias, d_skip, ssd_norm_w, h0)
    out = jnp.concatenate([attn.reshape(n, L, NA_WIDTH), y_ssd], axis=-1) @ w_out
    return out, k, v, h_fin


def short_conv_mixer(h, w_in, conv_w, w_out):
    gb, gc, xv = jnp.split(h @ w_in, 3, axis=-1)
    return (gb * centred_dwconv(gc * xv, conv_w)) @ w_out


def swiglu(h, wg, wu, wd):
    return (jax.nn.silu(h @ wg) * (h @ wu)) @ wd


def setup_inputs(seed: int = 0) -> dict:
    key = jax.random.key(seed)
    ks = jax.random.split(key, 32)
    f32 = jnp.float32
    D = D_MODEL

    def nrm(k, shape, scale):
        return jax.random.normal(k, shape, f32) * scale

    dt0 = jnp.exp(jax.random.uniform(ks[20], (N_EVEN, 2, SSD_HEADS), f32,
                                     minval=math.log(1e-3), maxval=math.log(1e-1)))
    return {
        'x_prompt': nrm(ks[0], (BATCH, SEQ, D), 1.0),
        'x_sample': nrm(ks[1], (DEC_BATCH, DEC_SEQ, D), 1.0),
        'cache_k': nrm(ks[2], (DEC_BATCH, N_EVEN, PAST_LEN, NA_HEADS, NA_HEAD_DIM), 1.0),
        'cache_v': nrm(ks[3], (DEC_BATCH, N_EVEN, PAST_LEN, NA_HEADS, NA_HEAD_DIM), 1.0),
        'state_ssm': nrm(ks[4], (DEC_BATCH, N_EVEN, 2, SSD_HEADS, SSD_HEAD_DIM, SSD_STATE), 0.1),
        'c': nrm(ks[5], (DEC_BATCH, D), 1.0),
        'c_ctx': nrm(ks[6], (D,), 1.0),
        'norm1_w': 1.0 + nrm(ks[7], (DEPTH, D), 0.02),
        'norm2_w': 1.0 + nrm(ks[8], (DEPTH, D), 0.02),
        'w_mod': nrm(ks[9], (DEPTH, D, 6 * D), 0.5 * D ** -0.5),
        'b_mod': nrm(ks[10], (DEPTH, 6 * D), 0.02),
        'w_in_even': nrm(ks[11], (N_EVEN, D, EVEN_IN), D ** -0.5),
        'w_out_even': nrm(ks[12], (N_EVEN, EVEN_MIX, D), EVEN_MIX ** -0.5),
        'q_norm_w': 1.0 + nrm(ks[13], (N_EVEN, NA_HEAD_DIM), 0.02),
        'k_norm_w': 1.0 + nrm(ks[14], (N_EVEN, NA_HEAD_DIM), 0.02),
        'na_rel_bias': nrm(ks[15], (N_EVEN, NA_HEADS, 2 * NA_ROWS - 1, 2 * NA_COLS - 1), 0.2),
        'ssd_conv_w': nrm(ks[16], (N_EVEN, SSD_CONV_W, SSD_CONV_DIM), SSD_CONV_W ** -0.5),
        'ssd_conv_b': nrm(ks[17], (N_EVEN, SSD_CONV_DIM), 0.02),
        'ssd_a_log': jnp.log(jax.random.uniform(ks[18], (N_EVEN, 2, SSD_HEADS), f32, minval=1.0, maxval=16.0)),
        'ssd_dt_bias': dt0 + jnp.log(-jnp.expm1(-dt0)),
        'ssd_d': 1.0 + nrm(ks[19], (N_EVEN, SSD_HEADS), 0.1),
        'ssd_norm_w': 1.0 + nrm(ks[21], (N_EVEN, SSD_INNER), 0.02),
        'w_in_odd': nrm(ks[22], (N_ODD, D, 3 * SC_WIDTH), D ** -0.5),
        'sc_conv_w': nrm(ks[23], (N_ODD, SC_CONV_W, SC_WIDTH), SC_CONV_W ** -0.5),
        'w_out_odd': nrm(ks[24], (N_ODD, SC_WIDTH, D), SC_WIDTH ** -0.5),
        'w_ffn_gate': nrm(ks[25], (DEPTH, D, D_FF), D ** -0.5),
        'w_ffn_up': nrm(ks[26], (DEPTH, D, D_FF), D ** -0.5),
        'w_ffn_down': nrm(ks[27], (DEPTH, D_FF, D), D_FF ** -0.5),
    }


def reference(x_prompt, x_sample, cache_k, cache_v, state_ssm, c, c_ctx, norm1_w, norm2_w, w_mod, b_mod,
              w_in_even, w_out_even, q_norm_w, k_norm_w, na_rel_bias, ssd_conv_w, ssd_conv_b, ssd_a_log,
              ssd_dt_bias, ssd_d, ssd_norm_w, w_in_odd, sc_conv_w, w_out_odd, w_ffn_gate, w_ffn_up,
              w_ffn_down):
    xp = x_prompt
    xs = x_sample
    cond_ctx = c_ctx[None, :]
    new_k, new_v, new_h = [], [], []
    for l in range(DEPTH):
        p_sh1, p_sc1, p_g1, p_sh2, p_sc2, p_g2 = adaln(cond_ctx, w_mod[l], b_mod[l])
        s_sh1, s_sc1, s_g1, s_sh2, s_sc2, s_g2 = adaln(c, w_mod[l], b_mod[l])
        hp = modulate(xp, norm1_w[l], p_sh1, p_sc1)
        hs = modulate(xs, norm1_w[l], s_sh1, s_sc1)
        j = l // 2
        if l % 2 == 0:
            ew = (w_in_even[j], w_out_even[j], q_norm_w[j], k_norm_w[j], na_rel_bias[j], ssd_conv_w[j],
                  ssd_conv_b[j], ssd_a_log[j], ssd_dt_bias[j], ssd_d[j], ssd_norm_w[j])
            out_p, k_j, v_j, h_j = even_mixer(hp, *ew)
            out_s = even_mixer(hs, *ew, ctx_k=cache_k[:, j], ctx_v=cache_v[:, j], ssm_h0=state_ssm[:, j])[0]
            new_k.append(k_j)
            new_v.append(v_j)
            new_h.append(h_j)
        else:
            out_p = short_conv_mixer(hp, w_in_odd[j], sc_conv_w[j], w_out_odd[j])
            out_s = short_conv_mixer(hs, w_in_odd[j], sc_conv_w[j], w_out_odd[j])
        xp = xp + p_g1 * out_p
        xs = xs + s_g1 * out_s
        xp = xp + p_g2 * swiglu(modulate(xp, norm2_w[l], p_sh2, p_sc2), w_ffn_gate[l], w_ffn_up[l], w_ffn_down[l])
        xs = xs + s_g2 * swiglu(modulate(xs, norm2_w[l], s_sh2, s_sc2), w_ffn_gate[l], w_ffn_up[l], w_ffn_down[l])
    new_cache_k = jnp.stack(new_k, axis=1)
    new_cache_v = jnp.stack(new_v, axis=1)
    new_state_ssm = jnp.stack(new_h, axis=1)
    return (xp, xs, new_cache_k, new_cache_v, new_state_ssm)
```

```python
import functools
import math

import numpy as np
import jax
import jax.numpy as jnp
from jax import lax
from jax.experimental import pallas as pl
from jax.experimental.pallas import tpu as pltpu

D_MODEL = 4096
BATCH, SEQ = 32, 256
DEC_BATCH, DEC_SEQ = 4, 4096
PAST_LEN = 256
DEPTH = 4
GRID_W = 64
GRID_ROWS = DEC_SEQ // GRID_W
NA_HEADS, NA_HEAD_DIM = 16, 128
NA_WIDTH = NA_HEADS * NA_HEAD_DIM
NA_ROWS, NA_COLS = 8, 16
SSD_HEAD_DIM = 64
SSD_INNER = D_MODEL // 2
SSD_HEADS = SSD_INNER // SSD_HEAD_DIM
SSD_GROUPS = 4
SSD_HPG = SSD_HEADS // SSD_GROUPS
SSD_STATE = 128
SSD_CONV_W = 5
SSD_CHUNK = 128
SSD_CONV_DIM = SSD_INNER + 2 * SSD_GROUPS * SSD_STATE
SSD_GROUP_W = SSD_HPG * SSD_HEAD_DIM
EVEN_MAIN = 3 * NA_WIDTH + SSD_INNER + SSD_CONV_DIM
SC_WIDTH = D_MODEL
SC_CONV_W = 3
D_FF = 11008
D_FF_PAD = 11264
EPS = 1e-6

N_PROMPT = BATCH * SEQ
N_SAMPLE = DEC_BATCH * DEC_SEQ
N_TOK = N_PROMPT + N_SAMPLE

LANES = 128
SUBLANES = 8
VMEM_LIMIT = 56 * 1024 * 1024
TM = 512
TN = 1024
ROW_TILE = 256
HALO = SUBLANES
NA_QROWS = 4
NA_KROWS = NA_QROWS + NA_ROWS
NEG = -1e30

F32 = jnp.float32
BF16 = jnp.bfloat16


def _cparams(sem):
    return pltpu.CompilerParams(dimension_semantics=sem, vmem_limit_bytes=VMEM_LIMIT)


def _silu(x):
    return x / (1.0 + jnp.exp(-x))


def _softplus(x):
    return jnp.maximum(x, 0.0) + jnp.log(1.0 + jnp.exp(-jnp.abs(x)))


def _cond_row(i, tm):
    n_p = N_PROMPT // tm
    per = DEC_SEQ // tm
    return jnp.where(i < n_p, 0, 1 + (i - n_p) // per)


def _adaln_body(c_ref, w_ref, b_ref, o_ref):
    a = _silu(c_ref[...]).astype(BF16)
    o_ref[...] = jnp.dot(a, w_ref[...].astype(BF16), preferred_element_type=F32) + b_ref[...]


def adaln_table(cond8, w_mod, b_mod):
    depth, d, n6 = w_mod.shape
    tn = 512
    out = pl.pallas_call(
        _adaln_body,
        grid=(depth, n6 // tn),
        in_specs=[
            pl.BlockSpec((8, d), lambda l, j: (0, 0)),
            pl.BlockSpec((None, d, tn), lambda l, j: (l, 0, j)),
            pl.BlockSpec((None, 1, tn), lambda l, j: (l, 0, j)),
        ],
        out_specs=pl.BlockSpec((None, 8, tn), lambda l, j: (l, 0, j)),
        out_shape=jax.ShapeDtypeStruct((depth, 8, n6), F32),
        compiler_params=_cparams(("parallel", "parallel")),
        name="adaln_table",
    )(cond8, w_mod, b_mod.reshape(depth, 1, n6))
    return out.reshape(depth, 8, 6, 1, d)


def _modnorm(x_ref, g_ref, sh_ref, sc_ref, h_ref):
    x = x_ref[...]
    ms = jnp.mean(x * x, axis=-1, keepdims=True)
    y = x * lax.rsqrt(ms + EPS) * g_ref[...]
    h_ref[...] = (y * (1.0 + sc_ref[...]) + sh_ref[...]).astype(BF16)


def _proj_body(x_ref, g_ref, sh_ref, sc_ref, w_ref, o_ref, h_ref):
    @pl.when(pl.program_id(1) == 0)
    def _():
        _modnorm(x_ref, g_ref, sh_ref, sc_ref, h_ref)

    o_ref[...] = jnp.dot(h_ref[...], w_ref[...], preferred_element_type=F32).astype(o_ref.dtype)


def _mod_specs(layer, which_shift, tm):
    d = D_MODEL
    return [
        pl.BlockSpec((None, None, None, 1, d), lambda i, j: (layer, _cond_row(i, tm), which_shift, 0, 0)),
        pl.BlockSpec((None, None, None, 1, d), lambda i, j: (layer, _cond_row(i, tm), which_shift + 1, 0, 0)),
    ]


def modnorm_proj(x, norm_w, mods, layer, which_shift, w, out_dtype, tn=TN, tm=TM):
    m, d = x.shape
    n = w.shape[1]
    return pl.pallas_call(
        _proj_body,
        grid=(m // tm, n // tn),
        in_specs=[
            pl.BlockSpec((tm, d), lambda i, j: (i, 0)),
            pl.BlockSpec((1, d), lambda i, j: (0, 0)),
            *_mod_specs(layer, which_shift, tm),
            pl.BlockSpec((d, tn), lambda i, j: (0, j)),
        ],
        out_specs=pl.BlockSpec((tm, tn), lambda i, j: (i, j)),
        out_shape=jax.ShapeDtypeStruct((m, n), out_dtype),
        scratch_shapes=[pltpu.VMEM((tm, d), BF16)],
        compiler_params=_cparams(("parallel", "arbitrary")),
        name="modnorm_proj",
    )(x, norm_w.reshape(1, d), mods, mods, w)


def _swiglu_body(x_ref, g_ref, sh_ref, sc_ref, wg_ref, wu_ref, o_ref, h_ref):
    @pl.when(pl.program_id(1) == 0)
    def _():
        _modnorm(x_ref, g_ref, sh_ref, sc_ref, h_ref)

    h = h_ref[...]
    a = jnp.dot(h, wg_ref[...], preferred_element_type=F32)
    b = jnp.dot(h, wu_ref[...], preferred_element_type=F32)
    o_ref[...] = (_silu(a) * b).astype(o_ref.dtype)


def modnorm_swiglu(x, norm_w, mods, layer, wg, wu, tn=512, tm=TM):
    m, d = x.shape
    n = wg.shape[1]
    return pl.pallas_call(
        _swiglu_body,
        grid=(m // tm, n // tn),
        in_specs=[
            pl.BlockSpec((tm, d), lambda i, j: (i, 0)),
            pl.BlockSpec((1, d), lambda i, j: (0, 0)),
            *_mod_specs(layer, 3, tm),
            pl.BlockSpec((d, tn), lambda i, j: (0, j)),
            pl.BlockSpec((d, tn), lambda i, j: (0, j)),
        ],
        out_specs=pl.BlockSpec((tm, tn), lambda i, j: (i, j)),
        out_shape=jax.ShapeDtypeStruct((m, n), BF16),
        scratch_shapes=[pltpu.VMEM((tm, d), BF16)],
        compiler_params=_cparams(("parallel", "arbitrary")),
        name="modnorm_swiglu",
    )(x, norm_w.reshape(1, d), mods, mods, wg, wu)


def _resid_body(a_ref, w_ref, x_ref, g_ref, o_ref, acc_ref, *, nk):
    k = pl.program_id(2)
    part = jnp.dot(a_ref[...], w_ref[...], preferred_element_type=F32)

    if nk == 1:
        o_ref[...] = x_ref[...] + g_ref[...] * part
    else:
        @pl.when(k == 0)
        def _():
            acc_ref[...] = part

        @pl.when(jnp.logical_and(k > 0, k < nk - 1))
        def _():
            acc_ref[...] += part

        @pl.when(k == nk - 1)
        def _():
            o_ref[...] = x_ref[...] + g_ref[...] * (acc_ref[...] + part)


def gated_residual_proj(a, w, x, mods, layer, which_gate, tk, tm=1024, tn=TN):
    m, kdim = a.shape
    n = w.shape[1]
    nk = kdim // tk
    return pl.pallas_call(
        functools.partial(_resid_body, nk=nk),
        grid=(m // tm, n // tn, nk),
        in_specs=[
            pl.BlockSpec((tm, tk), lambda i, j, k: (i, k)),
            pl.BlockSpec((tk, tn), lambda i, j, k: (k, j)),
            pl.BlockSpec((tm, tn), lambda i, j, k: (i, j)),
            pl.BlockSpec((None, None, None, 1, tn),
                         lambda i, j, k: (layer, _cond_row(i, tm), which_gate, 0, j)),
        ],
        out_specs=pl.BlockSpec((tm, tn), lambda i, j, k: (i, j)),
        out_shape=jax.ShapeDtypeStruct((m, n), F32),
        scratch_shapes=[pltpu.VMEM((tm, tn), F32)],
        compiler_params=_cparams(("parallel", "parallel", "arbitrary")),
        name="gated_residual_proj",
    )(a, w, x, mods)


def _head_rms(x, w):
    ms = jnp.mean(x * x, axis=-1, keepdims=True)
    return x * lax.rsqrt(ms + EPS) * w


def _nt_dot(a, b):
    return lax.dot_general(a, b, (((1,), (1,)), ((), ())), preferred_element_type=F32)


def _dense_attn_body(q_ref, k_ref, v_ref, qw_ref, kw_ref, o_ref, kn_ref, *, heads):
    scale = 1.0 / math.sqrt(NA_HEAD_DIM)
    for hh in range(heads):
        sl = slice(hh * NA_HEAD_DIM, (hh + 1) * NA_HEAD_DIM)
        qn = _head_rms(q_ref[:, sl], qw_ref[...]) * scale
        kn = _head_rms(k_ref[:, sl], kw_ref[...])
        kn_ref[:, sl] = kn
        s = _nt_dot(qn.astype(BF16), kn.astype(BF16))
        e = jnp.exp(s - jnp.max(s, axis=-1, keepdims=True))
        o = jnp.dot(e.astype(BF16), v_ref[:, sl].astype(BF16), preferred_element_type=F32)
        o_ref[:, sl] = (o / jnp.sum(e, axis=-1, keepdims=True)).astype(o_ref.dtype)


def dense_attention(proj, q_norm_w, k_norm_w):
    heads = 4
    bw = heads * NA_HEAD_DIM
    nb = NA_WIDTH // bw
    return pl.pallas_call(
        functools.partial(_dense_attn_body, heads=heads),
        grid=(BATCH, nb),
        in_specs=[
            pl.BlockSpec((SEQ, bw), lambda s, h: (s, h)),
            pl.BlockSpec((SEQ, bw), lambda s, h: (s, nb + h)),
            pl.BlockSpec((SEQ, bw), lambda s, h: (s, 2 * nb + h)),
            pl.BlockSpec((1, NA_HEAD_DIM), lambda s, h: (0, 0)),
            pl.BlockSpec((1, NA_HEAD_DIM), lambda s, h: (0, 0)),
        ],
        out_specs=[
            pl.BlockSpec((SEQ, bw), lambda s, h: (s, h)),
            pl.BlockSpec((SEQ, bw), lambda s, h: (s, h)),
        ],
        out_shape=[
            jax.ShapeDtypeStruct((N_PROMPT, NA_WIDTH), BF16),
            jax.ShapeDtypeStruct((N_PROMPT, NA_WIDTH), F32),
        ],
        compiler_params=_cparams(("parallel", "parallel")),
        name="dense_attention",
    )(proj, proj, proj, q_norm_w.reshape(1, -1), k_norm_w.reshape(1, -1))


def _na_window_start(blk):
    return jnp.clip(NA_QROWS * blk - NA_ROWS // 2, 0, GRID_ROWS - NA_KROWS)


def _na_attn_body(q_ref, k_ref, v_ref, ck_ref, cv_ref, bias_ref, qw_ref, kw_ref, o_ref):
    scale = 1.0 / math.sqrt(NA_HEAD_DIM)
    blk = pl.program_id(2)
    w0 = pl.multiple_of(_na_window_start(blk) * GRID_W, GRID_W)
    nk = NA_KROWS * GRID_W
    kwin = k_ref[pl.ds(w0, nk), :]
    vwin = v_ref[pl.ds(w0, nk), :]
    qn = (_head_rms(q_ref[...], qw_ref[...]) * scale).astype(BF16)
    kn = _head_rms(kwin, kw_ref[...]).astype(BF16)
    s_loc = _nt_dot(qn, kn) + bias_ref[...]
    s_ctx = _nt_dot(qn, ck_ref[...].astype(BF16))
    m = jnp.maximum(jnp.max(s_loc, axis=-1, keepdims=True), jnp.max(s_ctx, axis=-1, keepdims=True))
    e_loc = jnp.exp(s_loc - m)
    e_ctx = jnp.exp(s_ctx - m)
    den = jnp.sum(e_loc, axis=-1, keepdims=True) + jnp.sum(e_ctx, axis=-1, keepdims=True)
    o = (jnp.dot(e_loc.astype(BF16), vwin.astype(BF16), preferred_element_type=F32)
         + jnp.dot(e_ctx.astype(BF16), cv_ref[...].astype(BF16), preferred_element_type=F32))
    o_ref[...] = (o / den).astype(o_ref.dtype)


def _na_bias_table(rpb):
    nblk = GRID_ROWS // NA_QROWS
    tabs = []
    for blk in (0, 1, nblk - 1):
        w0 = int(np.clip(NA_QROWS * blk - NA_ROWS // 2, 0, GRID_ROWS - NA_KROWS))
        qr = (NA_QROWS * blk + np.arange(NA_QROWS))[:, None, None, None]
        qc = np.arange(GRID_W)[None, :, None, None]
        kr = (w0 + np.arange(NA_KROWS))[None, None, :, None]
        kc = np.arange(GRID_W)[None, None, None, :]
        rs = np.clip(qr - NA_ROWS // 2, 0, GRID_ROWS - NA_ROWS)
        cs = np.clip(qc - NA_COLS // 2, 0, GRID_W - NA_COLS)
        valid = (kr >= rs) & (kr < rs + NA_ROWS) & (kc >= cs) & (kc < cs + NA_COLS)
        dr = np.clip(kr - qr + NA_ROWS - 1, 0, 2 * NA_ROWS - 2)
        dc = np.clip(kc - qc + NA_COLS - 1, 0, 2 * NA_COLS - 2)
        shape = (NA_QROWS, GRID_W, NA_KROWS, GRID_W)
        dr = np.broadcast_to(dr, shape).reshape(NA_QROWS * GRID_W, NA_KROWS * GRID_W)
        dc = np.broadcast_to(dc, shape).reshape(NA_QROWS * GRID_W, NA_KROWS * GRID_W)
        valid = np.broadcast_to(valid, shape).reshape(NA_QROWS * GRID_W, NA_KROWS * GRID_W)
        tabs.append(jnp.where(valid[None], rpb[:, dr, dc], NEG))
    return jnp.stack(tabs, axis=1)


def neighbourhood_attention(proj, cache_k4, cache_v4, j, rpb, q_norm_w, k_norm_w):
    nblk = GRID_ROWS // NA_QROWS
    tq = NA_QROWS * GRID_W
    tk = NA_KROWS * GRID_W
    bias = _na_bias_table(rpb)
    qrow0 = N_PROMPT // tq
    krow0 = N_PROMPT // DEC_SEQ

    def variant(blk):
        return jnp.where(blk == 0, 0, jnp.where(blk == nblk - 1, 2, 1))

    hd = NA_HEAD_DIM
    return pl.pallas_call(
        _na_attn_body,
        grid=(DEC_BATCH, NA_HEADS, nblk),
        in_specs=[
            pl.BlockSpec((tq, hd), lambda b, h, t: (qrow0 + b * nblk + t, h)),
            pl.BlockSpec((DEC_SEQ, hd), lambda b, h, t: (krow0 + b, NA_HEADS + h)),
            pl.BlockSpec((DEC_SEQ, hd), lambda b, h, t: (krow0 + b, 2 * NA_HEADS + h)),
            pl.BlockSpec((None, None, PAST_LEN, hd), lambda b, h, t: (b, j, 0, h)),
            pl.BlockSpec((None, None, PAST_LEN, hd), lambda b, h, t: (b, j, 0, h)),
            pl.BlockSpec((None, None, tq, tk), lambda b, h, t: (h, variant(t), 0, 0)),
            pl.BlockSpec((1, hd), lambda b, h, t: (0, 0)),
            pl.BlockSpec((1, hd), lambda b, h, t: (0, 0)),
        ],
        out_specs=pl.BlockSpec((tq, hd), lambda b, h, t: (b * nblk + t, h)),
        out_shape=jax.ShapeDtypeStruct((N_SAMPLE, NA_WIDTH), BF16),
        compiler_params=_cparams(("parallel", "parallel", "arbitrary")),
        name="neighbourhood_attention",
    )(proj, proj, proj, cache_k4, cache_v4, bias, q_norm_w.reshape(1, -1), k_norm_w.reshape(1, -1))


def _tile_seq_pos(c, rt):
    n_p = N_PROMPT // rt
    pp, ps = SEQ // rt, DEC_SEQ // rt
    pos = jnp.where(c < n_p, c % pp, (c - n_p) % ps)
    per = jnp.where(c < n_p, pp, ps)
    return pos == 0, pos == per - 1


def _halo_specs(rt, width, col_of):
    per = rt // HALO
    last = N_TOK // HALO - 1
    return [
        pl.BlockSpec((HALO, width), lambda c, j: (jnp.maximum(c * per - 1, 0), col_of(j))),
        pl.BlockSpec((rt, width), lambda c, j: (c, col_of(j))),
        pl.BlockSpec((HALO, width), lambda c, j: (jnp.minimum((c + 1) * per, last), col_of(j))),
    ]


def _ssd_conv_body(prev_ref, cur_ref, next_ref, w_ref, b_ref, o_ref, ext_ref, *, rt):
    first, last = _tile_seq_pos(pl.program_id(0), rt)
    ext_ref[0:HALO] = jnp.where(first, 0.0, prev_ref[...])
    ext_ref[HALO:HALO + rt] = cur_ref[...]
    ext_ref[HALO + rt:] = jnp.where(last, 0.0, next_ref[...])
    pad = SSD_CONV_W // 2
    acc = b_ref[...] + ext_ref[HALO - pad:HALO - pad + rt] * w_ref[0:1]
    for k in range(1, SSD_CONV_W):
        acc = acc + ext_ref[HALO - pad + k:HALO - pad + k + rt] * w_ref[k:k + 1]
    o_ref[...] = _silu(acc)


def ssd_conv(proj, conv_w, conv_b, rt=ROW_TILE, width=1024):
    col0 = (3 * NA_WIDTH + SSD_INNER) // width
    return pl.pallas_call(
        functools.partial(_ssd_conv_body, rt=rt),
        grid=(N_TOK // rt, SSD_CONV_DIM // width),
        in_specs=[
            *_halo_specs(rt, width, lambda j: col0 + j),
            pl.BlockSpec((SSD_CONV_W, width), lambda c, j: (0, j)),
            pl.BlockSpec((1, width), lambda c, j: (0, j)),
        ],
        out_specs=pl.BlockSpec((rt, width), lambda c, j: (c, j)),
        out_shape=jax.ShapeDtypeStruct((N_TOK, SSD_CONV_DIM), F32),
        scratch_shapes=[pltpu.VMEM((rt + 2 * HALO, width), F32)],
        compiler_params=_cparams(("parallel", "parallel")),
        name="ssd_conv",
    )(proj, proj, proj, conv_w, conv_b.reshape(1, -1))


def _short_conv_body(gb_ref, cp_ref, cc_ref, cn_ref, xp_ref, xc_ref, xn_ref, w_ref, o_ref, ext_ref, *, rt):
    first, last = _tile_seq_pos(pl.program_id(0), rt)
    ext_ref[0:HALO] = jnp.where(first, 0.0, cp_ref[...] * xp_ref[...])
    ext_ref[HALO:HALO + rt] = cc_ref[...] * xc_ref[...]
    ext_ref[HALO + rt:] = jnp.where(last, 0.0, cn_ref[...] * xn_ref[...])
    pad = SC_CONV_W // 2
    acc = ext_ref[HALO - pad:HALO - pad + rt] * w_ref[0:1]
    for k in range(1, SC_CONV_W):
        acc = acc + ext_ref[HALO - pad + k:HALO - pad + k + rt] * w_ref[k:k + 1]
    o_ref[...] = (gb_ref[...] * acc).astype(o_ref.dtype)


def short_conv_gate(proj, conv_w, rt=ROW_TILE, width=1024):
    nb = SC_WIDTH // width
    return pl.pallas_call(
        functools.partial(_short_conv_body, rt=rt),
        grid=(N_TOK // rt, nb),
        in_specs=[
            pl.BlockSpec((rt, width), lambda c, j: (c, j)),
            *_halo_specs(rt, width, lambda j: nb + j),
            *_halo_specs(rt, width, lambda j: 2 * nb + j),
            pl.BlockSpec((SC_CONV_W, width), lambda c, j: (0, j)),
        ],
        out_specs=pl.BlockSpec((rt, width), lambda c, j: (c, j)),
        out_shape=jax.ShapeDtypeStruct((N_TOK, SC_WIDTH), BF16),
        scratch_shapes=[pltpu.VMEM((rt + 2 * HALO, width), F32)],
        compiler_params=_cparams(("parallel", "parallel")),
        name="short_conv_gate",
    )(proj, proj, proj, proj, proj, proj, proj, conv_w)


def _chunk_cumsum(x, reverse):
    rows = lax.broadcasted_iota(jnp.int32, x.shape, 0)
    n = x.shape[0]
    k = 1
    while k < n:
        if reverse:
            x = x + jnp.where(rows < n - k, pltpu.roll(x, n - k, 0), 0.0)
        else:
            x = x + jnp.where(rows >= k, pltpu.roll(x, k, 0), 0.0)
        k *= 2
    return x


def _dt_and_logdecay(dt_ref, bias_ref, a_ref):
    dtv = _softplus(dt_ref[...] + bias_ref[...])
    dta = dtv * a_ref[...]
    lane = lax.broadcasted_iota(jnp.int32, dta.shape, 1)
    acum = jnp.where(lane < SSD_HEADS, _chunk_cumsum(dta, False), _chunk_cumsum(dta, True))
    return dtv, acum


def _pair_cols(m, c0, lane):
    shape = (m.shape[0], LANES)
    return jnp.where(lane[:m.shape[0]] < SSD_HEAD_DIM,
                     jnp.broadcast_to(m[:, c0:c0 + 1], shape),
                     jnp.broadcast_to(m[:, c0 + 1:c0 + 2], shape))


def _ssd_state_body(xf_ref, bf_ref, dtf_ref, xb_ref, bb_ref, dtb_ref, h0_ref, bias_ref, a_ref,
                    hpf_ref, hpb_ref, hfin_ref, h_ref, *, nc):
    t = pl.program_id(1)

    @pl.when(t == 0)
    def _():
        h_ref[...] = h0_ref[...]

    lane = lax.broadcasted_iota(jnp.int32, (SSD_CHUNK, LANES), 1)
    for d, (x_ref, b_ref, dt_ref, hp_ref) in enumerate(
            ((xf_ref, bf_ref, dtf_ref, hpf_ref), (xb_ref, bb_ref, dtb_ref, hpb_ref))):
        dtv, acum = _dt_and_logdecay(dt_ref, bias_ref, a_ref)
        edge = acum[SSD_CHUNK - 1:SSD_CHUNK] if d == 0 else acum[0:1]
        s = dtv * jnp.exp(edge - acum)
        cd = jnp.exp(edge)
        hp_ref[...] = h_ref[d].astype(BF16)
        for g in range(SSD_GROUPS):
            bt = jnp.transpose(b_ref[:, g * SSD_STATE:(g + 1) * SSD_STATE]).astype(BF16)
            for pr in range(SSD_HPG // 2):
                head = g * SSD_HPG + 2 * pr
                c0 = d * SSD_HEADS + head
                cols = slice(head * SSD_HEAD_DIM, head * SSD_HEAD_DIM + LANES)
                xd = (x_ref[:, cols] * _pair_cols(s, c0, lane)).astype(BF16)
                st = jnp.dot(bt, xd, preferred_element_type=F32)
                h_ref[d, :, cols] = h_ref[d, :, cols] * _pair_cols(cd, c0, lane) + st

    @pl.when(t == nc - 1)
    def _():
        hfin_ref[...] = h_ref[...]


def ssd_states(xbc, dt_raw, h0t, dt_bias, a_neg, row0, nseq, nc):
    c0 = row0 // SSD_CHUNK
    hp = SSD_INNER
    bcol = SSD_INNER // (SSD_GROUPS * SSD_STATE)

    def fwd(s, t):
        return c0 + s * nc + t

    def bwd(s, t):
        return c0 + s * nc + (nc - 1 - t)

    return pl.pallas_call(
        functools.partial(_ssd_state_body, nc=nc),
        grid=(nseq, nc),
        in_specs=[
            pl.BlockSpec((SSD_CHUNK, hp), lambda s, t: (fwd(s, t), 0)),
            pl.BlockSpec((SSD_CHUNK, SSD_GROUPS * SSD_STATE), lambda s, t: (fwd(s, t), bcol)),
            pl.BlockSpec((SSD_CHUNK, LANES), lambda s, t: (fwd(s, t), 0)),
            pl.BlockSpec((SSD_CHUNK, hp), lambda s, t: (bwd(s, t), 0)),
            pl.BlockSpec((SSD_CHUNK, SSD_GROUPS * SSD_STATE), lambda s, t: (bwd(s, t), bcol)),
            pl.BlockSpec((SSD_CHUNK, LANES), lambda s, t: (bwd(s, t), 0)),
            pl.BlockSpec((None, 2, SSD_STATE, hp), lambda s, t: (s, 0, 0, 0)),
            pl.BlockSpec((1, LANES), lambda s, t: (0, 0)),
            pl.BlockSpec((1, LANES), lambda s, t: (0, 0)),
        ],
        out_specs=[
            pl.BlockSpec((None, None, SSD_STATE, hp), lambda s, t: (s, t, 0, 0)),
            pl.BlockSpec((None, None, SSD_STATE, hp), lambda s, t: (s, nc - 1 - t, 0, 0)),
            pl.BlockSpec((None, 2, SSD_STATE, hp), lambda s, t: (s, 0, 0, 0)),
        ],
        out_shape=[
            jax.ShapeDtypeStruct((nseq, nc, SSD_STATE, hp), BF16),
            jax.ShapeDtypeStruct((nseq, nc, SSD_STATE, hp), BF16),
            jax.ShapeDtypeStruct((nseq, 2, SSD_STATE, hp), F32),
        ],
        scratch_shapes=[pltpu.VMEM((2, SSD_STATE, hp), F32)],
        compiler_params=_cparams(("parallel", "arbitrary")),
        name="ssd_states",
    )(xbc, xbc, dt_raw, xbc, xbc, dt_raw, h0t, dt_bias, a_neg)


def _ssd_out_body(x_ref, b_ref, c_ref, z_ref, dt_ref, hpf_ref, hpb_ref, bias_ref, a_ref, dsk_ref, nw_ref,
                  o_ref, g_ref):
    dtv, acum = _dt_and_logdecay(dt_ref, bias_ref, a_ref)
    acum_t = jnp.transpose(acum)
    dt_t = jnp.transpose(dtv)
    e = jnp.exp(acum)
    shape = (SSD_CHUNK, SSD_CHUNK)
    rows = lax.broadcasted_iota(jnp.int32, shape, 0)
    lane = lax.broadcasted_iota(jnp.int32, shape, 1)
    lower = rows >= lane
    upper = rows <= lane

    def mix_weights(cb, head):
        cf, cbk = head, SSD_HEADS + head
        seg_f = jnp.broadcast_to(acum[:, cf:cf + 1], shape) - acum_t[cf:cf + 1, :]
        seg_b = jnp.broadcast_to(acum[:, cbk:cbk + 1], shape) - acum_t[cbk:cbk + 1, :]
        lf = jnp.exp(jnp.where(lower, seg_f, -jnp.inf)) * dt_t[cf:cf + 1, :]
        lb = jnp.exp(jnp.where(upper, seg_b, -jnp.inf)) * dt_t[cbk:cbk + 1, :]
        return (cb * (lf + lb)).astype(BF16)

    for g in range(SSD_GROUPS):
        gcols = slice(g * SSD_GROUP_W, (g + 1) * SSD_GROUP_W)
        bg = b_ref[:, g * SSD_STATE:(g + 1) * SSD_STATE].astype(BF16)
        cg = c_ref[:, g * SSD_STATE:(g + 1) * SSD_STATE].astype(BF16)
        cb = _nt_dot(cg, bg)
        yoff_f = jnp.dot(cg, hpf_ref[:, gcols], preferred_element_type=F32)
        yoff_b = jnp.dot(cg, hpb_ref[:, gcols], preferred_element_type=F32)
        for pr in range(SSD_HPG // 2):
            head = g * SSD_HPG + 2 * pr
            cols = slice(head * SSD_HEAD_DIM, head * SSD_HEAD_DIM + LANES)
            pcols = slice(pr * LANES, (pr + 1) * LANES)
            xp = x_ref[:, cols]
            xb = xp.astype(BF16)
            ya = jnp.dot(mix_weights(cb, head), xb, preferred_element_type=F32)
            yb = jnp.dot(mix_weights(cb, head + 1), xb, preferred_element_type=F32)
            y = jnp.where(lane < SSD_HEAD_DIM, ya, yb)
            y = y + _pair_cols(e, head, lane) * yoff_f[:, pcols]
            y = y + _pair_cols(e, SSD_HEADS + head, lane) * yoff_b[:, pcols]
            y = y + xp * dsk_ref[:, cols]
            g_ref[:, cols] = y * _silu(z_ref[:, cols])
        gg = g_ref[:, gcols]
        ms = jnp.mean(gg * gg, axis=-1, keepdims=True)
        o_ref[:, gcols] = (gg * lax.rsqrt(ms + EPS) * nw_ref[:, gcols]).astype(o_ref.dtype)


def ssd_outputs(xbc, proj, dt_raw, hpf, hpb, dt_bias, a_neg, d_skip, norm_w, row0, nseq, nc):
    c0 = row0 // SSD_CHUNK
    hp = SSD_INNER
    gw = SSD_GROUPS * SSD_STATE
    bcol = SSD_INNER // gw
    zcol = 3 * NA_WIDTH // SSD_INNER

    def row(s, t):
        return c0 + s * nc + t

    return pl.pallas_call(
        _ssd_out_body,
        grid=(nseq, nc),
        in_specs=[
            pl.BlockSpec((SSD_CHUNK, hp), lambda s, t: (row(s, t), 0)),
            pl.BlockSpec((SSD_CHUNK, gw), lambda s, t: (row(s, t), bcol)),
            pl.BlockSpec((SSD_CHUNK, gw), lambda s, t: (row(s, t), bcol + 1)),
            pl.BlockSpec((SSD_CHUNK, hp), lambda s, t: (row(s, t), zcol)),
            pl.BlockSpec((SSD_CHUNK, LANES), lambda s, t: (row(s, t), 0)),
            pl.BlockSpec((None, None, SSD_STATE, hp), lambda s, t: (s, t, 0, 0)),
            pl.BlockSpec((None, None, SSD_STATE, hp), lambda s, t: (s, t, 0, 0)),
            pl.BlockSpec((1, LANES), lambda s, t: (0, 0)),
            pl.BlockSpec((1, LANES), lambda s, t: (0, 0)),
            pl.BlockSpec((1, hp), lambda s, t: (0, 0)),
            pl.BlockSpec((1, hp), lambda s, t: (0, 0)),
        ],
        out_specs=pl.BlockSpec((SSD_CHUNK, hp), lambda s, t: (s * nc + t, 0)),
        out_shape=jax.ShapeDtypeStruct((nseq * nc * SSD_CHUNK, hp), BF16),
        scratch_shapes=[pltpu.VMEM((SSD_CHUNK, hp), F32)],
        compiler_params=_cparams(("parallel", "parallel")),
        name="ssd_outputs",
    )(xbc, xbc, xbc, proj, dt_raw, hpf, hpb, dt_bias, a_neg, d_skip, norm_w)


def _pad_lanes(v):
    return jnp.pad(v.reshape(1, -1), ((0, 0), (0, LANES - v.size)))


def ssd_mixer(proj, dt_raw, state_h0, conv_w, conv_b, a_log, dt_bias, d_skip, norm_w):
    xbc = ssd_conv(proj, conv_w, conv_b)
    bias = _pad_lanes(dt_bias)
    a_neg = _pad_lanes(-jnp.exp(a_log))
    dsk = jnp.repeat(d_skip, SSD_HEAD_DIM).reshape(1, -1)
    nw = norm_w.reshape(1, -1)

    def to_t(h):
        return jnp.transpose(h, (0, 1, 4, 2, 3)).reshape(h.shape[0], 2, SSD_STATE, SSD_INNER)

    ys, fins = [], []
    for row0, nseq, nc, h0 in ((0, BATCH, SEQ // SSD_CHUNK, None),
                               (N_PROMPT, DEC_BATCH, DEC_SEQ // SSD_CHUNK, state_h0)):
        h0t = jnp.zeros((nseq, 2, SSD_STATE, SSD_INNER), F32) if h0 is None else to_t(h0)
        hpf, hpb, hfin = ssd_states(xbc, dt_raw, h0t, bias, a_neg, row0, nseq, nc)
        ys.append(ssd_outputs(xbc, proj, dt_raw, hpf, hpb, bias, a_neg, dsk, nw, row0, nseq, nc))
        fins.append(hfin)
    hfin_p = fins[0].reshape(BATCH, 2, SSD_STATE, SSD_HEADS, SSD_HEAD_DIM)
    return jnp.concatenate(ys, axis=0), jnp.transpose(hfin_p, (0, 1, 3, 4, 2))


def kernel(x_prompt, x_sample, cache_k, cache_v, state_ssm, c, c_ctx, norm1_w, norm2_w, w_mod, b_mod,
           w_in_even, w_out_even, q_norm_w, k_norm_w, na_rel_bias, ssd_conv_w, ssd_conv_b, ssd_a_log,
           ssd_dt_bias, ssd_d, ssd_norm_w, w_in_odd, sc_conv_w, w_out_odd, w_ffn_gate, w_ffn_up,
           w_ffn_down):
    d = D_MODEL
    x = jnp.concatenate([x_prompt.reshape(N_PROMPT, d), x_sample.reshape(N_SAMPLE, d)], axis=0)
    cond8 = jnp.concatenate([c_ctx[None, :], c, jnp.zeros((8 - 1 - DEC_BATCH, d), F32)], axis=0)
    mods = adaln_table(cond8, w_mod, b_mod)
    n_even = w_in_even.shape[0]
    cache_k4 = cache_k.reshape(DEC_BATCH, n_even, PAST_LEN, NA_WIDTH)
    cache_v4 = cache_v.reshape(DEC_BATCH, n_even, PAST_LEN, NA_WIDTH)

    new_k, new_v, new_h = [], [], []
    for l in range(DEPTH):
        j = l // 2
        if l % 2 == 0:
            w_main = w_in_even[j, :, :EVEN_MAIN].astype(BF16)
            w_dt = jnp.pad(w_in_even[j, :, EVEN_MAIN:], ((0, 0), (0, LANES - 2 * SSD_HEADS))).astype(BF16)
            proj = modnorm_proj(x, norm1_w[l], mods, l, 0, w_main, F32)
            dt_raw = modnorm_proj(x, norm1_w[l], mods, l, 0, w_dt, F32, tn=LANES)
            attn_p, k_j = dense_attention(proj, q_norm_w[j], k_norm_w[j])
            attn_s = neighbourhood_attention(proj, cache_k4, cache_v4, j, na_rel_bias[j],
                                             q_norm_w[j], k_norm_w[j])
            y_ssd, h_j = ssd_mixer(proj, dt_raw, state_ssm[:, j], ssd_conv_w[j], ssd_conv_b[j],
                                   ssd_a_log[j].reshape(-1), ssd_dt_bias[j].reshape(-1), ssd_d[j],
                                   ssd_norm_w[j])
            mix = jnp.concatenate([jnp.concatenate([attn_p, attn_s], axis=0), y_ssd], axis=1)
            w_out = w_out_even[j].astype(BF16)
            new_k.append(k_j.reshape(BATCH, SEQ, NA_HEADS, NA_HEAD_DIM))
            new_v.append(proj[:N_PROMPT, 2 * NA_WIDTH:3 * NA_WIDTH].reshape(BATCH, SEQ, NA_HEADS, NA_HEAD_DIM))
            new_h.append(h_j)
        else:
            proj = modnorm_proj(x, norm1_w[l], mods, l, 0, w_in_odd[j].astype(BF16), F32)
            mix = short_conv_gate(proj, sc_conv_w[j])
            w_out = w_out_odd[j].astype(BF16)
        x = gated_residual_proj(mix, w_out, x, mods, l, 2, tk=d // 2)
        pad = D_FF_PAD - D_FF
        wg = jnp.pad(w_ffn_gate[l], ((0, 0), (0, pad))).astype(BF16)
        wu = jnp.pad(w_ffn_up[l], ((0, 0), (0, pad))).astype(BF16)
        wd = jnp.pad(w_ffn_down[l], ((0, pad), (0, 0))).astype(BF16)
        hidden = modnorm_swiglu(x, norm2_w[l], mods, l, wg, wu)
        x = gated_residual_proj(hidden, wd, x, mods, l, 5, tk=D_FF_PAD // 4)

    y_prompt = x[:N_PROMPT].reshape(BATCH, SEQ, d)
    y_sample = x[N_PROMPT:].reshape(DEC_BATCH, DEC_SEQ, d)
    return (y_prompt, y_sample, jnp.stack(new_k, axis=1), jnp.stack(new_v, axis=1), jnp.stack(new_h, axis=1))
```

```python
import functools
import math

import numpy as np
import jax
import jax.numpy as jnp
from jax import lax
from jax.experimental import pallas as pl
from jax.experimental.pallas import tpu as pltpu

D_MODEL = 4096
BATCH, SEQ = 32, 256
DEC_BATCH, DEC_SEQ = 4, 4096
PAST_LEN = 256
DEPTH = 4
GRID_W = 64
GRID_ROWS = DEC_SEQ // GRID_W
NA_HEADS, NA_HEAD_DIM = 16, 128
NA_WIDTH = NA_HEADS * NA_HEAD_DIM
NA_ROWS, NA_COLS = 8, 16
SSD_HEAD_DIM = 64
SSD_INNER = D_MODEL // 2
SSD_HEADS = SSD_INNER // SSD_HEAD_DIM
SSD_GROUPS = 4
SSD_HPG = SSD_HEADS // SSD_GROUPS
SSD_STATE = 128
SSD_CONV_W = 5
SSD_CHUNK = 128
SSD_CONV_DIM = SSD_INNER + 2 * SSD_GROUPS * SSD_STATE
SSD_GROUP_W = SSD_HPG * SSD_HEAD_DIM
EVEN_MAIN = 3 * NA_WIDTH + SSD_INNER + SSD_CONV_DIM
SC_WIDTH = D_MODEL
SC_CONV_W = 3
EPS = 1e-6

N_PROMPT = BATCH * SEQ
N_SAMPLE = DEC_BATCH * DEC_SEQ
N_TOK = N_PROMPT + N_SAMPLE

LANES = 128
SUBLANES = 8
VMEM_LIMIT = 56 * 1024 * 1024
ROW_TILE = 256
HALO = SUBLANES
HALO_BF16 = 2 * SUBLANES
NA_QROWS = 4
NA_KROWS = NA_QROWS + NA_ROWS
NEG = -1e30

F32 = jnp.float32
BF16 = jnp.bfloat16


def _cparams(sem):
    return pltpu.CompilerParams(dimension_semantics=sem, vmem_limit_bytes=VMEM_LIMIT)


def _silu(x):
    return x / (1.0 + jnp.exp(-x))


def _softplus(x):
    return jnp.maximum(x, 0.0) + jnp.log(1.0 + jnp.exp(-jnp.abs(x)))


def _cond_row(i, tm):
    n_p = N_PROMPT // tm
    per = DEC_SEQ // tm
    return jnp.where(i < n_p, 0, 1 + (i - n_p) // per)


def _adaln_body(c_ref, w_ref, b_ref, o_ref):
    a = _silu(c_ref[...]).astype(BF16)
    o_ref[...] = jnp.dot(a, w_ref[...].astype(BF16), preferred_element_type=F32) + b_ref[...]


def adaln_table(cond8, w_mod, b_mod):
    depth, d, n6 = w_mod.shape
    tn = 512
    out = pl.pallas_call(
        _adaln_body,
        grid=(depth, n6 // tn),
        in_specs=[
            pl.BlockSpec((8, d), lambda l, j: (0, 0)),
            pl.BlockSpec((None, d, tn), lambda l, j: (l, 0, j)),
            pl.BlockSpec((None, 1, tn), lambda l, j: (l, 0, j)),
        ],
        out_specs=pl.BlockSpec((None, 8, tn), lambda l, j: (l, 0, j)),
        out_shape=jax.ShapeDtypeStruct((depth, 8, n6), F32),
        compiler_params=_cparams(("parallel", "parallel")),
        name="adaln_table",
    )(cond8, w_mod, b_mod.reshape(depth, 1, n6))
    return out.reshape(depth, 8, 6, 1, d)


def _modnorm_body(x_ref, g_ref, sh_ref, sc_ref, h_ref):
    x = x_ref[...]
    ms = jnp.mean(x * x, axis=-1, keepdims=True)
    y = x * lax.rsqrt(ms + EPS) * g_ref[...]
    h_ref[...] = (y * (1.0 + sc_ref[...]) + sh_ref[...]).astype(h_ref.dtype)


def modnorm(x, norm_w, mods, layer, which_shift, tm=ROW_TILE):
    m, d = x.shape

    def mod_spec(which):
        return pl.BlockSpec((None, None, None, 1, d), lambda i: (layer, _cond_row(i, tm), which, 0, 0))

    return pl.pallas_call(
        _modnorm_body,
        grid=(m // tm,),
        in_specs=[
            pl.BlockSpec((tm, d), lambda i: (i, 0)),
            pl.BlockSpec((1, d), lambda i: (0, 0)),
            mod_spec(which_shift),
            mod_spec(which_shift + 1),
        ],
        out_specs=pl.BlockSpec((tm, d), lambda i: (i, 0)),
        out_shape=jax.ShapeDtypeStruct((m, d), BF16),
        compiler_params=_cparams(("parallel",)),
        name="modnorm",
    )(x, norm_w.reshape(1, d), mods, mods)


def _cast_weight_once(w_ref, wb_ref):
    @pl.when(pl.program_id(1) == 0)
    def _():
        wb_ref[...] = w_ref[...].astype(BF16)


def _proj_body(h_ref, w_ref, o_ref, wb_ref):
    _cast_weight_once(w_ref, wb_ref)
    o_ref[...] = jnp.dot(h_ref[...], wb_ref[...], preferred_element_type=F32).astype(o_ref.dtype)


def proj(h, w3, widx, n, out_dtype, tn=512, tm=1024):
    m, d = h.shape
    return pl.pallas_call(
        _proj_body,
        grid=(pl.cdiv(n, tn), m // tm),
        in_specs=[
            pl.BlockSpec((tm, d), lambda j, i: (i, 0)),
            pl.BlockSpec((None, d, tn), lambda j, i: (widx, 0, j)),
        ],
        out_specs=pl.BlockSpec((tm, tn), lambda j, i: (i, j)),
        out_shape=jax.ShapeDtypeStruct((m, n), out_dtype),
        scratch_shapes=[pltpu.VMEM((d, tn), BF16)],
        compiler_params=_cparams(("parallel", "arbitrary")),
        name="proj",
    )(h, w3)


def _swiglu_body(h_ref, wg_ref, wu_ref, o_ref, wgb_ref, wub_ref):
    _cast_weight_once(wg_ref, wgb_ref)
    _cast_weight_once(wu_ref, wub_ref)
    h = h_ref[...]
    a = jnp.dot(h, wgb_ref[...], preferred_element_type=F32)
    b = jnp.dot(h, wub_ref[...], preferred_element_type=F32)
    o_ref[...] = (_silu(a) * b).astype(o_ref.dtype)


def swiglu(h, wg3, wu3, layer, tn=512, tm=512):
    m, d = h.shape
    n = wg3.shape[2]
    wspec = pl.BlockSpec((None, d, tn), lambda j, i: (layer, 0, j))
    return pl.pallas_call(
        _swiglu_body,
        grid=(pl.cdiv(n, tn), m // tm),
        in_specs=[pl.BlockSpec((tm, d), lambda j, i: (i, 0)), wspec, wspec],
        out_specs=pl.BlockSpec((tm, tn), lambda j, i: (i, j)),
        out_shape=jax.ShapeDtypeStruct((m, n), BF16),
        scratch_shapes=[pltpu.VMEM((d, tn), BF16), pltpu.VMEM((d, tn), BF16)],
        compiler_params=_cparams(("parallel", "arbitrary")),
        name="swiglu",
    )(h, wg3, wu3)


def _mixer_out_body(a_ref, w_ref, x_ref, g_ref, o_ref, wb_ref):
    _cast_weight_once(w_ref, wb_ref)
    o_ref[...] = x_ref[...] + g_ref[...] * jnp.dot(a_ref[...], wb_ref[...], preferred_element_type=F32)


def mixer_out_proj(a, w3, widx, x, mods, layer, tn=512, tm=1024):
    m, kdim = a.shape
    n = w3.shape[2]
    return pl.pallas_call(
        _mixer_out_body,
        grid=(n // tn, m // tm),
        in_specs=[
            pl.BlockSpec((tm, kdim), lambda j, i: (i, 0)),
            pl.BlockSpec((None, kdim, tn), lambda j, i: (widx, 0, j)),
            pl.BlockSpec((tm, tn), lambda j, i: (i, j)),
            pl.BlockSpec((None, None, None, 1, tn), lambda j, i: (layer, _cond_row(i, tm), 2, 0, j)),
        ],
        out_specs=pl.BlockSpec((tm, tn), lambda j, i: (i, j)),
        out_shape=jax.ShapeDtypeStruct((m, n), F32),
        scratch_shapes=[pltpu.VMEM((kdim, tn), BF16)],
        compiler_params=_cparams(("parallel", "arbitrary")),
        name="mixer_out_proj",
    )(a, w3, x, mods)


def _ffn_down_body(a_ref, w_ref, x_ref, g_ref, o_ref):
    o_ref[...] = x_ref[...] + g_ref[...] * jnp.dot(a_ref[...], w_ref[...], preferred_element_type=F32)


def ffn_down_proj(a, w, x, mods, layer, row0=0, nrows=None, tn=512, tm=512):
    kdim = a.shape[1]
    n = w.shape[1]
    nrows = a.shape[0] if nrows is None else nrows
    i0 = row0 // tm
    return pl.pallas_call(
        _ffn_down_body,
        grid=(nrows // tm, n // tn),
        in_specs=[
            pl.BlockSpec((tm, kdim), lambda i, j: (i0 + i, 0)),
            pl.BlockSpec((kdim, tn), lambda i, j: (0, j)),
            pl.BlockSpec((tm, tn), lambda i, j: (i0 + i, j)),
            pl.BlockSpec((None, None, None, 1, tn), lambda i, j: (layer, _cond_row(i0 + i, tm), 5, 0, j)),
        ],
        out_specs=pl.BlockSpec((tm, tn), lambda i, j: (i, j)),
        out_shape=jax.ShapeDtypeStruct((nrows, n), F32),
        compiler_params=_cparams(("parallel", "parallel")),
        name="ffn_down_proj",
    )(a, w, x, mods)


def _head_rms(x, w):
    ms = jnp.mean(x * x, axis=-1, keepdims=True)
    return x * lax.rsqrt(ms + EPS) * w


def _nt_dot(a, b):
    return lax.dot_general(a, b, (((1,), (1,)), ((), ())), preferred_element_type=F32)


def _dense_attn_body(q_ref, k_ref, v_ref, qw_ref, kw_ref, o_ref, kn_ref, *, heads):
    scale = 1.0 / math.sqrt(NA_HEAD_DIM)
    for hh in range(heads):
        sl = slice(hh * NA_HEAD_DIM, (hh + 1) * NA_HEAD_DIM)
        qn = _head_rms(q_ref[:, sl], qw_ref[...]) * scale
        kn = _head_rms(k_ref[:, sl], kw_ref[...])
        kn_ref[:, sl] = kn
        s = _nt_dot(qn.astype(BF16), kn.astype(BF16))
        e = jnp.exp(s - jnp.max(s, axis=-1, keepdims=True))
        o = jnp.dot(e.astype(BF16), v_ref[:, sl].astype(BF16), preferred_element_type=F32)
        o_ref[:, sl] = (o / jnp.sum(e, axis=-1, keepdims=True)).astype(o_ref.dtype)


def dense_attention(proj, q_norm_w, k_norm_w):
    heads = 4
    bw = heads * NA_HEAD_DIM
    nb = NA_WIDTH // bw
    return pl.pallas_call(
        functools.partial(_dense_attn_body, heads=heads),
        grid=(BATCH, nb),
        in_specs=[
            pl.BlockSpec((SEQ, bw), lambda s, h: (s, h)),
            pl.BlockSpec((SEQ, bw), lambda s, h: (s, nb + h)),
            pl.BlockSpec((SEQ, bw), lambda s, h: (s, 2 * nb + h)),
            pl.BlockSpec((1, NA_HEAD_DIM), lambda s, h: (0, 0)),
            pl.BlockSpec((1, NA_HEAD_DIM), lambda s, h: (0, 0)),
        ],
        out_specs=[
            pl.BlockSpec((SEQ, bw), lambda s, h: (s, h)),
            pl.BlockSpec((SEQ, bw), lambda s, h: (s, h)),
        ],
        out_shape=[
            jax.ShapeDtypeStruct((N_TOK, D_MODEL), BF16),
            jax.ShapeDtypeStruct((N_PROMPT, NA_WIDTH), F32),
        ],
        compiler_params=_cparams(("parallel", "parallel")),
        name="dense_attention",
    )(proj, proj, proj, q_norm_w.reshape(1, -1), k_norm_w.reshape(1, -1))


def _na_window_start(blk):
    return jnp.clip(NA_QROWS * blk - NA_ROWS // 2, 0, GRID_ROWS - NA_KROWS)


def _na_attn_body(q_ref, k_ref, v_ref, ck_ref, cv_ref, bias_ref, qw_ref, kw_ref, mix_ref, o_ref,
                  kn_ref, vb_ref, ckb_ref, cvb_ref):
    del mix_ref
    scale = 1.0 / math.sqrt(NA_HEAD_DIM)
    blk = pl.program_id(2)

    @pl.when(blk == 0)
    def _():
        kn_ref[...] = _head_rms(k_ref[...], kw_ref[...]).astype(BF16)
        vb_ref[...] = v_ref[...].astype(BF16)
        ckb_ref[...] = ck_ref[...].astype(BF16)
        cvb_ref[...] = cv_ref[...].astype(BF16)

    w0 = pl.multiple_of(_na_window_start(blk) * GRID_W, GRID_W)
    nk = NA_KROWS * GRID_W
    qn = (_head_rms(q_ref[...], qw_ref[...]) * scale).astype(BF16)
    s_loc = _nt_dot(qn, kn_ref[pl.ds(w0, nk), :]) + bias_ref[...]
    s_ctx = _nt_dot(qn, ckb_ref[...])
    m = jnp.maximum(jnp.max(s_loc, axis=-1, keepdims=True), jnp.max(s_ctx, axis=-1, keepdims=True))
    e_loc = jnp.exp(s_loc - m)
    e_ctx = jnp.exp(s_ctx - m)
    den = jnp.sum(e_loc, axis=-1, keepdims=True) + jnp.sum(e_ctx, axis=-1, keepdims=True)
    o = (jnp.dot(e_loc.astype(BF16), vb_ref[pl.ds(w0, nk), :], preferred_element_type=F32)
         + jnp.dot(e_ctx.astype(BF16), cvb_ref[...], preferred_element_type=F32))
    o_ref[...] = (o / den).astype(o_ref.dtype)


def _na_bias_table(rpb):
    nh, nblk, w = rpb.shape[0], GRID_ROWS // NA_QROWS, GRID_W
    v = jnp.pad(rpb, ((0, 0), (0, 0), (w - NA_COLS, w - NA_COLS + 1)))
    toe = jnp.tile(v, (1, 1, w))[:, :, :w * (2 * w - 1)].reshape(nh, -1, w, 2 * w - 1)[..., w - 1:]
    qc, kc = np.arange(w)[:, None], np.arange(w)[None, :]
    cs = np.clip(qc - NA_COLS // 2, 0, w - NA_COLS)
    toe = jnp.where(((kc >= cs) & (kc < cs + NA_COLS))[None, None], toe, NEG)
    masked = jnp.full((nh, w, w), NEG, F32)
    tabs = []
    for blk in (0, 1, nblk - 1):
        w0 = int(np.clip(NA_QROWS * blk - NA_ROWS // 2, 0, GRID_ROWS - NA_KROWS))
        qrows = []
        for a in range(NA_QROWS):
            qr = NA_QROWS * blk + a
            rs = int(np.clip(qr - NA_ROWS // 2, 0, GRID_ROWS - NA_ROWS))
            krows = [toe[:, w0 + t - qr + NA_ROWS - 1] if rs <= w0 + t < rs + NA_ROWS else masked
                     for t in range(NA_KROWS)]
            qrows.append(jnp.stack(krows, axis=2))
        tabs.append(jnp.stack(qrows, axis=1).reshape(nh, NA_QROWS * w, NA_KROWS * w))
    return jnp.stack(tabs, axis=1)


def neighbourhood_attention(proj, cache_k4, cache_v4, j, rpb, q_norm_w, k_norm_w, mix):
    nblk = GRID_ROWS // NA_QROWS
    tq = NA_QROWS * GRID_W
    tk = NA_KROWS * GRID_W
    bias = _na_bias_table(rpb)
    qrow0 = N_PROMPT // tq
    krow0 = N_PROMPT // DEC_SEQ

    def variant(blk):
        return jnp.where(blk == 0, 0, jnp.where(blk == nblk - 1, 2, 1))

    hd = NA_HEAD_DIM
    return pl.pallas_call(
        _na_attn_body,
        grid=(DEC_BATCH, NA_HEADS, nblk),
        in_specs=[
            pl.BlockSpec((tq, hd), lambda b, h, t: (qrow0 + b * nblk + t, h)),
            pl.BlockSpec((DEC_SEQ, hd), lambda b, h, t: (krow0 + b, NA_HEADS + h)),
            pl.BlockSpec((DEC_SEQ, hd), lambda b, h, t: (krow0 + b, 2 * NA_HEADS + h)),
            pl.BlockSpec((None, None, PAST_LEN, hd), lambda b, h, t: (b, j, 0, h)),
            pl.BlockSpec((None, None, PAST_LEN, hd), lambda b, h, t: (b, j, 0, h)),
            pl.BlockSpec((None, None, tq, tk), lambda b, h, t: (h, variant(t), 0, 0)),
            pl.BlockSpec((1, hd), lambda b, h, t: (0, 0)),
            pl.BlockSpec((1, hd), lambda b, h, t: (0, 0)),
            pl.BlockSpec(memory_space=pl.ANY),
        ],
        out_specs=pl.BlockSpec((tq, hd), lambda b, h, t: (qrow0 + b * nblk + t, h)),
        out_shape=jax.ShapeDtypeStruct(mix.shape, mix.dtype),
        input_output_aliases={8: 0},
        scratch_shapes=[pltpu.VMEM((DEC_SEQ, hd), BF16), pltpu.VMEM((DEC_SEQ, hd), BF16),
                        pltpu.VMEM((PAST_LEN, hd), BF16), pltpu.VMEM((PAST_LEN, hd), BF16)],
        compiler_params=_cparams(("parallel", "parallel", "arbitrary")),
        name="neighbourhood_attention",
    )(proj, proj, proj, cache_k4, cache_v4, bias, q_norm_w.reshape(1, -1), k_norm_w.reshape(1, -1), mix)


def _tile_seq_pos(c, rt):
    n_p = N_PROMPT // rt
    pp, ps = SEQ // rt, DEC_SEQ // rt
    pos = jnp.where(c < n_p, c % pp, (c - n_p) % ps)
    per = jnp.where(c < n_p, pp, ps)
    return pos == 0, pos == per - 1


def _halo_specs(rt, width, col_of, halo):
    per = rt // halo
    last = N_TOK // halo - 1
    return [
        pl.BlockSpec((halo, width), lambda c, j: (jnp.maximum(c * per - 1, 0), col_of(j))),
        pl.BlockSpec((rt, width), lambda c, j: (c, col_of(j))),
        pl.BlockSpec((halo, width), lambda c, j: (jnp.minimum((c + 1) * per, last), col_of(j))),
    ]


def _ssd_conv_body(prev_ref, cur_ref, next_ref, w_ref, b_ref, o_ref, ext_ref, *, rt):
    first, last = _tile_seq_pos(pl.program_id(0), rt)
    ext_ref[0:HALO] = jnp.where(first, 0.0, prev_ref[...])
    ext_ref[HALO:HALO + rt] = cur_ref[...]
    ext_ref[HALO + rt:] = jnp.where(last, 0.0, next_ref[...])
    pad = SSD_CONV_W // 2
    acc = b_ref[...] + ext_ref[HALO - pad:HALO - pad + rt] * w_ref[0:1]
    for k in range(1, SSD_CONV_W):
        acc = acc + ext_ref[HALO - pad + k:HALO - pad + k + rt] * w_ref[k:k + 1]
    o_ref[...] = _silu(acc)


def ssd_conv(proj, conv_w, conv_b, rt=ROW_TILE, width=1024):
    col0 = (3 * NA_WIDTH + SSD_INNER) // width
    return pl.pallas_call(
        functools.partial(_ssd_conv_body, rt=rt),
        grid=(N_TOK // rt, SSD_CONV_DIM // width),
        in_specs=[
            *_halo_specs(rt, width, lambda j: col0 + j, HALO),
            pl.BlockSpec((SSD_CONV_W, width), lambda c, j: (0, j)),
            pl.BlockSpec((1, width), lambda c, j: (0, j)),
        ],
        out_specs=pl.BlockSpec((rt, width), lambda c, j: (c, j)),
        out_shape=jax.ShapeDtypeStruct((N_TOK, SSD_CONV_DIM), F32),
        scratch_shapes=[pltpu.VMEM((rt + 2 * HALO, width), F32)],
        compiler_params=_cparams(("parallel", "parallel")),
        name="ssd_conv",
    )(proj, proj, proj, conv_w, conv_b.reshape(1, -1))


def _short_conv_body(gb_ref, cp_ref, cc_ref, cn_ref, xp_ref, xc_ref, xn_ref, w_ref, o_ref, ext_ref, *, rt):
    first, last = _tile_seq_pos(pl.program_id(0), rt)

    def prod(c_ref, x_ref):
        return c_ref[...].astype(F32) * x_ref[...].astype(F32)

    h = HALO_BF16
    ext_ref[0:h] = jnp.where(first, 0.0, prod(cp_ref, xp_ref))
    ext_ref[h:h + rt] = prod(cc_ref, xc_ref)
    ext_ref[h + rt:] = jnp.where(last, 0.0, prod(cn_ref, xn_ref))
    pad = SC_CONV_W // 2
    acc = ext_ref[h - pad:h - pad + rt] * w_ref[0:1]
    for k in range(1, SC_CONV_W):
        acc = acc + ext_ref[h - pad + k:h - pad + k + rt] * w_ref[k:k + 1]
    o_ref[...] = (gb_ref[...].astype(F32) * acc).astype(o_ref.dtype)


def short_conv_gate(proj, conv_w, rt=ROW_TILE, width=1024):
    nb = SC_WIDTH // width
    return pl.pallas_call(
        functools.partial(_short_conv_body, rt=rt),
        grid=(N_TOK // rt, nb),
        in_specs=[
            pl.BlockSpec((rt, width), lambda c, j: (c, j)),
            *_halo_specs(rt, width, lambda j: nb + j, HALO_BF16),
            *_halo_specs(rt, width, lambda j: 2 * nb + j, HALO_BF16),
            pl.BlockSpec((SC_CONV_W, width), lambda c, j: (0, j)),
        ],
        out_specs=pl.BlockSpec((rt, width), lambda c, j: (c, j)),
        out_shape=jax.ShapeDtypeStruct((N_TOK, SC_WIDTH), BF16),
        scratch_shapes=[pltpu.VMEM((rt + 2 * HALO_BF16, width), F32)],
        compiler_params=_cparams(("parallel", "parallel")),
        name="short_conv_gate",
    )(proj, proj, proj, proj, proj, proj, proj, conv_w)


def _chunk_cumsum(x, reverse):
    rows = lax.broadcasted_iota(jnp.int32, x.shape, 0)
    n = x.shape[0]
    k = 1
    while k < n:
        if reverse:
            x = x + jnp.where(rows < n - k, pltpu.roll(x, n - k, 0), 0.0)
        else:
            x = x + jnp.where(rows >= k, pltpu.roll(x, k, 0), 0.0)
        k *= 2
    return x


def _dt_and_logdecay(dt_ref, bias_ref, a_ref):
    dtv = _softplus(dt_ref[...] + bias_ref[...])
    dta = dtv * a_ref[...]
    lane = lax.broadcasted_iota(jnp.int32, dta.shape, 1)
    acum = jnp.where(lane < SSD_HEADS, _chunk_cumsum(dta, False), _chunk_cumsum(dta, True))
    return dtv, acum


def _pair_cols(m, c0, lane):
    shape = (m.shape[0], LANES)
    return jnp.where(lane[:m.shape[0]] < SSD_HEAD_DIM,
                     jnp.broadcast_to(m[:, c0:c0 + 1], shape),
                     jnp.broadcast_to(m[:, c0 + 1:c0 + 2], shape))


def _ssd_state_body(xf_ref, bf_ref, dtf_ref, xb_ref, bb_ref, dtb_ref, h0_ref, bias_ref, a_ref,
                    hpf_ref, hpb_ref, hfin_ref, h_ref, *, nc):
    t = pl.program_id(1)

    @pl.when(t == 0)
    def _():
        h_ref[...] = h0_ref[...]

    lane = lax.broadcasted_iota(jnp.int32, (SSD_CHUNK, LANES), 1)
    for d, (x_ref, b_ref, dt_ref, hp_ref) in enumerate(
            ((xf_ref, bf_ref, dtf_ref, hpf_ref), (xb_ref, bb_ref, dtb_ref, hpb_ref))):
        dtv, acum = _dt_and_logdecay(dt_ref, bias_ref, a_ref)
        edge = acum[SSD_CHUNK - 1:SSD_CHUNK] if d == 0 else acum[0:1]
        s = dtv * jnp.exp(edge - acum)
        cd = jnp.exp(edge)
        hp_ref[...] = h_ref[d].astype(BF16)
        for g in range(SSD_GROUPS):
            bt = jnp.transpose(b_ref[:, g * SSD_STATE:(g + 1) * SSD_STATE]).astype(BF16)
            for pr in range(SSD_HPG // 2):
                head = g * SSD_HPG + 2 * pr
                c0 = d * SSD_HEADS + head
                cols = slice(head * SSD_HEAD_DIM, head * SSD_HEAD_DIM + LANES)
                xd = (x_ref[:, cols] * _pair_cols(s, c0, lane)).astype(BF16)
                st = jnp.dot(bt, xd, preferred_element_type=F32)
                h_ref[d, :, cols] = h_ref[d, :, cols] * _pair_cols(cd, c0, lane) + st

    @pl.when(t == nc - 1)
    def _():
        hfin_ref[...] = h_ref[...]


def ssd_states(xbc, dt_raw, h0t, dt_bias, a_neg, row0, nseq, nc):
    c0 = row0 // SSD_CHUNK
    hp = SSD_INNER
    bcol = SSD_INNER // (SSD_GROUPS * SSD_STATE)

    def fwd(s, t):
        return c0 + s * nc + t

    def bwd(s, t):
        return c0 + s * nc + (nc - 1 - t)

    return pl.pallas_call(
        functools.partial(_ssd_state_body, nc=nc),
        grid=(nseq, nc),
        in_specs=[
            pl.BlockSpec((SSD_CHUNK, hp), lambda s, t: (fwd(s, t), 0)),
            pl.BlockSpec((SSD_CHUNK, SSD_GROUPS * SSD_STATE), lambda s, t: (fwd(s, t), bcol)),
            pl.BlockSpec((SSD_CHUNK, LANES), lambda s, t: (fwd(s, t), 0)),
            pl.BlockSpec((SSD_CHUNK, hp), lambda s, t: (bwd(s, t), 0)),
            pl.BlockSpec((SSD_CHUNK, SSD_GROUPS * SSD_STATE), lambda s, t: (bwd(s, t), bcol)),
            pl.BlockSpec((SSD_CHUNK, LANES), lambda s, t: (bwd(s, t), 0)),
            pl.BlockSpec((None, 2, SSD_STATE, hp), lambda s, t: (s, 0, 0, 0)),
            pl.BlockSpec((1, LANES), lambda s, t: (0, 0)),
            pl.BlockSpec((1, LANES), lambda s, t: (0, 0)),
        ],
        out_specs=[
            pl.BlockSpec((None, None, SSD_STATE, hp), lambda s, t: (s, t, 0, 0)),
            pl.BlockSpec((None, None, SSD_STATE, hp), lambda s, t: (s, nc - 1 - t, 0, 0)),
            pl.BlockSpec((None, 2, SSD_STATE, hp), lambda s, t: (s, 0, 0, 0)),
        ],
        out_shape=[
            jax.ShapeDtypeStruct((nseq, nc, SSD_STATE, hp), BF16),
            jax.ShapeDtypeStruct((nseq, nc, SSD_STATE, hp), BF16),
            jax.ShapeDtypeStruct((nseq, 2, SSD_STATE, hp), F32),
        ],
        scratch_shapes=[pltpu.VMEM((2, SSD_STATE, hp), F32)],
        compiler_params=_cparams(("parallel", "arbitrary")),
        name="ssd_states",
    )(xbc, xbc, dt_raw, xbc, xbc, dt_raw, h0t, dt_bias, a_neg)


def _ssd_out_body(x_ref, b_ref, c_ref, z_ref, dt_ref, hpf_ref, hpb_ref, bias_ref, a_ref, dsk_ref, nw_ref,
                  mix_ref, o_ref, g_ref):
    del mix_ref
    dtv, acum = _dt_and_logdecay(dt_ref, bias_ref, a_ref)
    acum_t = jnp.transpose(acum)
    dt_t = jnp.transpose(dtv)
    shape = (SSD_CHUNK, SSD_CHUNK)
    rows = lax.broadcasted_iota(jnp.int32, shape, 0)
    lane = lax.broadcasted_iota(jnp.int32, shape, 1)
    lower = rows >= lane
    upper = rows <= lane

    def head_terms(cb, head):
        cf, cbk = head, SSD_HEADS + head
        col_f = jnp.broadcast_to(acum[:, cf:cf + 1], shape)
        col_b = jnp.broadcast_to(acum[:, cbk:cbk + 1], shape)
        lf = jnp.exp(jnp.where(lower, col_f - acum_t[cf:cf + 1, :], -jnp.inf)) * dt_t[cf:cf + 1, :]
        lb = jnp.exp(jnp.where(upper, col_b - acum_t[cbk:cbk + 1, :], -jnp.inf)) * dt_t[cbk:cbk + 1, :]
        return (cb * (lf + lb)).astype(BF16), jnp.exp(col_f), jnp.exp(col_b)

    for g in range(SSD_GROUPS):
        gcols = slice(g * SSD_GROUP_W, (g + 1) * SSD_GROUP_W)
        bg = b_ref[:, g * SSD_STATE:(g + 1) * SSD_STATE].astype(BF16)
        cg = c_ref[:, g * SSD_STATE:(g + 1) * SSD_STATE].astype(BF16)
        cb = _nt_dot(cg, bg)
        yoff_f = jnp.dot(cg, hpf_ref[:, gcols], preferred_element_type=F32)
        yoff_b = jnp.dot(cg, hpb_ref[:, gcols], preferred_element_type=F32)
        for pr in range(SSD_HPG // 2):
            head = g * SSD_HPG + 2 * pr
            cols = slice(head * SSD_HEAD_DIM, head * SSD_HEAD_DIM + LANES)
            pcols = slice(pr * LANES, (pr + 1) * LANES)
            xp = x_ref[:, cols]
            xb = xp.astype(BF16)
            first = lane < SSD_HEAD_DIM
            w0, ef0, eb0 = head_terms(cb, head)
            w1, ef1, eb1 = head_terms(cb, head + 1)
            y = jnp.where(first, jnp.dot(w0, xb, preferred_element_type=F32),
                          jnp.dot(w1, xb, preferred_element_type=F32))
            y = y + jnp.where(first, ef0, ef1) * yoff_f[:, pcols]
            y = y + jnp.where(first, eb0, eb1) * yoff_b[:, pcols]
            y = y + xp * dsk_ref[:, cols]
            g_ref[:, cols] = y * _silu(z_ref[:, cols])
        gg = g_ref[:, gcols]
        ms = jnp.mean(gg * gg, axis=-1, keepdims=True)
        o_ref[:, gcols] = (gg * lax.rsqrt(ms + EPS) * nw_ref[:, gcols]).astype(o_ref.dtype)


def ssd_outputs(xbc, proj, dt_raw, hpf, hpb, dt_bias, a_neg, d_skip, norm_w, row0, nseq, nc, mix):
    c0 = row0 // SSD_CHUNK
    hp = SSD_INNER
    gw = SSD_GROUPS * SSD_STATE
    bcol = SSD_INNER // gw
    zcol = 3 * NA_WIDTH // SSD_INNER

    def row(s, t):
        return c0 + s * nc + t

    return pl.pallas_call(
        _ssd_out_body,
        grid=(nseq, nc),
        in_specs=[
            pl.BlockSpec((SSD_CHUNK, hp), lambda s, t: (row(s, t), 0)),
            pl.BlockSpec((SSD_CHUNK, gw), lambda s, t: (row(s, t), bcol)),
            pl.BlockSpec((SSD_CHUNK, gw), lambda s, t: (row(s, t), bcol + 1)),
            pl.BlockSpec((SSD_CHUNK, hp), lambda s, t: (row(s, t), zcol)),
            pl.BlockSpec((SSD_CHUNK, LANES), lambda s, t: (row(s, t), 0)),
            pl.BlockSpec((None, None, SSD_STATE, hp), lambda s, t: (s, t, 0, 0)),
            pl.BlockSpec((None, None, SSD_STATE, hp), lambda s, t: (s, t, 0, 0)),
            pl.BlockSpec((1, LANES), lambda s, t: (0, 0)),
            pl.BlockSpec((1, LANES), lambda s, t: (0, 0)),
            pl.BlockSpec((1, hp), lambda s, t: (0, 0)),
            pl.BlockSpec((1, hp), lambda s, t: (0, 0)),
            pl.BlockSpec(memory_space=pl.ANY),
        ],
        out_specs=pl.BlockSpec((SSD_CHUNK, hp), lambda s, t: (row(s, t), NA_WIDTH // hp)),
        out_shape=jax.ShapeDtypeStruct(mix.shape, mix.dtype),
        input_output_aliases={11: 0},
        scratch_shapes=[pltpu.VMEM((SSD_CHUNK, hp), F32)],
        compiler_params=_cparams(("parallel", "parallel")),
        name="ssd_outputs",
    )(xbc, xbc, xbc, proj, dt_raw, hpf, hpb, dt_bias, a_neg, d_skip, norm_w, mix)


def _pad_lanes(v):
    return jnp.pad(v.reshape(1, -1), ((0, 0), (0, LANES - v.size)))


def ssd_mixer(proj, dt_raw, state_h0, conv_w, conv_b, a_log, dt_bias, d_skip, norm_w, mix):
    xbc = ssd_conv(proj, conv_w, conv_b)
    bias = _pad_lanes(dt_bias)
    a_neg = _pad_lanes(-jnp.exp(a_log))
    dsk = jnp.repeat(d_skip, SSD_HEAD_DIM).reshape(1, -1)
    nw = norm_w.reshape(1, -1)

    def to_t(h):
        return jnp.transpose(h, (0, 1, 4, 2, 3)).reshape(h.shape[0], 2, SSD_STATE, SSD_INNER)

    fins = []
    for row0, nseq, nc, h0 in ((0, BATCH, SEQ // SSD_CHUNK, None),
                               (N_PROMPT, DEC_BATCH, DEC_SEQ // SSD_CHUNK, state_h0)):
        h0t = jnp.zeros((nseq, 2, SSD_STATE, SSD_INNER), F32) if h0 is None else to_t(h0)
        hpf, hpb, hfin = ssd_states(xbc, dt_raw, h0t, bias, a_neg, row0, nseq, nc)
        mix = ssd_outputs(xbc, proj, dt_raw, hpf, hpb, bias, a_neg, dsk, nw, row0, nseq, nc, mix)
        fins.append(hfin)
    hfin_p = fins[0].reshape(BATCH, 2, SSD_STATE, SSD_HEADS, SSD_HEAD_DIM)
    return mix, jnp.transpose(hfin_p, (0, 1, 3, 4, 2))


def kernel(x_prompt, x_sample, cache_k, cache_v, state_ssm, c, c_ctx, norm1_w, norm2_w, w_mod, b_mod,
           w_in_even, w_out_even, q_norm_w, k_norm_w, na_rel_bias, ssd_conv_w, ssd_conv_b, ssd_a_log,
           ssd_dt_bias, ssd_d, ssd_norm_w, w_in_odd, sc_conv_w, w_out_odd, w_ffn_gate, w_ffn_up,
           w_ffn_down):
    d = D_MODEL
    x = jnp.concatenate([x_prompt.reshape(N_PROMPT, d), x_sample.reshape(N_SAMPLE, d)], axis=0)
    cond8 = jnp.concatenate([c_ctx[None, :], c, jnp.zeros((8 - 1 - DEC_BATCH, d), F32)], axis=0)
    mods = adaln_table(cond8, w_mod, b_mod)
    n_even = w_in_even.shape[0]
    cache_k4 = cache_k.reshape(DEC_BATCH, n_even, PAST_LEN, NA_WIDTH)
    cache_v4 = cache_v.reshape(DEC_BATCH, n_even, PAST_LEN, NA_WIDTH)

    new_k, new_v, new_h = [], [], []
    for l in range(DEPTH):
        j = l // 2
        h = modnorm(x, norm1_w[l], mods, l, 0)
        if l % 2 == 0:
            w_dt = jnp.pad(w_in_even[j:j + 1, :, EVEN_MAIN:], ((0, 0), (0, 0), (0, LANES - 2 * SSD_HEADS)))
            pr = proj(h, w_in_even, j, EVEN_MAIN, F32)
            dt_raw = proj(h, w_dt, 0, LANES, F32, tn=LANES)
            mix, k_j = dense_attention(pr, q_norm_w[j], k_norm_w[j])
            mix = neighbourhood_attention(pr, cache_k4, cache_v4, j, na_rel_bias[j],
                                          q_norm_w[j], k_norm_w[j], mix)
            mix, h_j = ssd_mixer(pr, dt_raw, state_ssm[:, j], ssd_conv_w[j], ssd_conv_b[j],
                                 ssd_a_log[j].reshape(-1), ssd_dt_bias[j].reshape(-1), ssd_d[j],
                                 ssd_norm_w[j], mix)
            x = mixer_out_proj(mix, w_out_even, j, x, mods, l)
            new_k.append(k_j.reshape(BATCH, SEQ, NA_HEADS, NA_HEAD_DIM))
            new_v.append(pr[:N_PROMPT, 2 * NA_WIDTH:3 * NA_WIDTH].reshape(BATCH, SEQ, NA_HEADS, NA_HEAD_DIM))
            new_h.append(h_j)
        else:
            pr = proj(h, w_in_odd, j, 3 * SC_WIDTH, BF16)
            mix = short_conv_gate(pr, sc_conv_w[j])
            x = mixer_out_proj(mix, w_out_odd, j, x, mods, l)
        h = modnorm(x, norm2_w[l], mods, l, 3)
        hidden = swiglu(h, w_ffn_gate, w_ffn_up, l)
        w_down = w_ffn_down[l].astype(BF16)
        if l < DEPTH - 1:
            x = ffn_down_proj(hidden, w_down, x, mods, l)
        else:
            y_prompt = ffn_down_proj(hidden, w_down, x, mods, l, 0, N_PROMPT).reshape(BATCH, SEQ, d)
            y_sample = ffn_down_proj(hidden, w_down, x, mods, l, N_PROMPT, N_SAMPLE).reshape(DEC_BATCH, DEC_SEQ, d)

    return (y_prompt, y_sample, jnp.stack(new_k, axis=1), jnp.stack(new_v, axis=1), jnp.stack(new_h, axis=1))
```

```python
import functools
import math

import numpy as np
import jax
import jax.numpy as jnp
from jax import lax
from jax.experimental import pallas as pl
from jax.experimental.pallas import tpu as pltpu

D_MODEL = 4096
BATCH, SEQ = 32, 256
DEC_BATCH, DEC_SEQ = 4, 4096
PAST_LEN = 256
DEPTH = 4
GRID_W = 64
GRID_ROWS = DEC_SEQ // GRID_W
NA_HEADS, NA_HEAD_DIM = 16, 128
NA_WIDTH = NA_HEADS * NA_HEAD_DIM
NA_ROWS, NA_COLS = 8, 16
SSD_HEAD_DIM = 64
SSD_INNER = D_MODEL // 2
SSD_HEADS = SSD_INNER // SSD_HEAD_DIM
SSD_GROUPS = 4
SSD_HPG = SSD_HEADS // SSD_GROUPS
SSD_STATE = 128
SSD_CONV_W = 5
SSD_CHUNK = 128
SSD_CONV_DIM = SSD_INNER + 2 * SSD_GROUPS * SSD_STATE
SSD_GROUP_W = SSD_HPG * SSD_HEAD_DIM
EVEN_MAIN = 3 * NA_WIDTH + SSD_INNER + SSD_CONV_DIM
SC_WIDTH = D_MODEL
SC_CONV_W = 3
EPS = 1e-6

N_PROMPT = BATCH * SEQ
N_SAMPLE = DEC_BATCH * DEC_SEQ
N_TOK = N_PROMPT + N_SAMPLE

LANES = 128
SUBLANES = 8
VMEM_LIMIT = 56 * 1024 * 1024
ROW_TILE = 256
HALO = SUBLANES
HALO_BF16 = 2 * SUBLANES
NA_QROWS = 4
NA_KROWS = NA_QROWS + NA_ROWS
NA_HEADS_PER_STEP = 2
NEG = -1e30

F32 = jnp.float32
BF16 = jnp.bfloat16


def _cparams(sem):
    return pltpu.CompilerParams(dimension_semantics=sem, vmem_limit_bytes=VMEM_LIMIT)


def _silu(x):
    return x / (1.0 + jnp.exp(-x))


def _softplus(x):
    return jnp.maximum(x, 0.0) + jnp.log(1.0 + jnp.exp(-jnp.abs(x)))


def _cond_row(i, tm):
    n_p = N_PROMPT // tm
    per = DEC_SEQ // tm
    return jnp.where(i < n_p, 0, 1 + (i - n_p) // per)


def _adaln_body(c_ref, w_ref, b_ref, o_ref):
    a = _silu(c_ref[...]).astype(BF16)
    o_ref[...] = jnp.dot(a, w_ref[...].astype(BF16), preferred_element_type=F32) + b_ref[...]


def adaln_table(cond8, w_mod, b_mod):
    depth, d, n6 = w_mod.shape
    tn = 512
    out = pl.pallas_call(
        _adaln_body,
        grid=(depth, n6 // tn),
        in_specs=[
            pl.BlockSpec((8, d), lambda l, j: (0, 0)),
            pl.BlockSpec((None, d, tn), lambda l, j: (l, 0, j)),
            pl.BlockSpec((None, 1, tn), lambda l, j: (l, 0, j)),
        ],
        out_specs=pl.BlockSpec((None, 8, tn), lambda l, j: (l, 0, j)),
        out_shape=jax.ShapeDtypeStruct((depth, 8, n6), F32),
        compiler_params=_cparams(("parallel", "parallel")),
        name="adaln_table",
    )(cond8, w_mod, b_mod.reshape(depth, 1, n6))
    return out.reshape(depth, 8, 6, 1, d)


def _row_part_specs(parts, tm, tn, ids):
    if len(parts) == 1:
        return [pl.BlockSpec((tm, tn), lambda *g: ids(*g))]
    n_p = N_PROMPT // tm
    return [
        pl.BlockSpec((tm, tn), lambda *g: (jnp.minimum(ids(*g)[0], n_p - 1), ids(*g)[1])),
        pl.BlockSpec((tm, tn), lambda *g: (jnp.maximum(ids(*g)[0] - n_p, 0), ids(*g)[1])),
    ]


def _modnorm_body(*refs, tm):
    *x_refs, g_ref, sh_ref, sc_ref, h_ref, gain_ref = refs
    gain_ref[...] = g_ref[...] * (1.0 + sc_ref[...])
    d = h_ref.shape[1]
    rows = 8 * SUBLANES
    chunk = 2 * LANES

    def run(x_ref):
        def body(r, carry):
            sl = pl.ds(pl.multiple_of(r * rows, rows), rows)
            acc = jnp.zeros((rows, LANES), F32)
            for c in range(0, d, LANES):
                xc = x_ref[sl, c:c + LANES]
                acc = acc + xc * xc
            rinv = lax.rsqrt(jnp.sum(acc, axis=-1, keepdims=True) / d + EPS)
            for c in range(0, d, chunk):
                cs = slice(c, c + chunk)
                h_ref[sl, cs] = (x_ref[sl, cs] * rinv * gain_ref[:, cs] + sh_ref[:, cs]).astype(h_ref.dtype)
            return carry

        lax.fori_loop(0, tm // rows, body, 0)

    if len(x_refs) == 1:
        run(x_refs[0])
    else:
        in_prompt = pl.program_id(0) < N_PROMPT // tm
        pl.when(in_prompt)(lambda: run(x_refs[0]))
        pl.when(jnp.logical_not(in_prompt))(lambda: run(x_refs[1]))


def modnorm(x_parts, norm_w, mods, layer, which_shift, tm=2 * ROW_TILE):
    d = D_MODEL

    def mod_spec(which):
        return pl.BlockSpec((None, None, None, 1, d), lambda i: (layer, _cond_row(i, tm), which, 0, 0))

    return pl.pallas_call(
        functools.partial(_modnorm_body, tm=tm),
        grid=(N_TOK // tm,),
        in_specs=[
            *_row_part_specs(x_parts, tm, d, lambda i: (i, 0)),
            pl.BlockSpec((1, d), lambda i: (0, 0)),
            mod_spec(which_shift),
            mod_spec(which_shift + 1),
        ],
        out_specs=pl.BlockSpec((tm, d), lambda i: (i, 0)),
        out_shape=jax.ShapeDtypeStruct((N_TOK, d), BF16),
        scratch_shapes=[pltpu.VMEM((1, d), F32)],
        compiler_params=_cparams(("parallel",)),
        name="modnorm",
    )(*x_parts, norm_w.reshape(1, d), mods, mods)


def _cast_weight_once(w_ref, wb_ref):
    @pl.when(pl.program_id(1) == 0)
    def _():
        wb_ref[...] = w_ref[...].astype(BF16)


def _proj_body(h_ref, w_ref, o_ref, wb_ref):
    _cast_weight_once(w_ref, wb_ref)
    o_ref[...] = jnp.dot(h_ref[...], wb_ref[...], preferred_element_type=F32).astype(o_ref.dtype)


def _weight_spec(kdim, tn, layer):
    return pl.BlockSpec((None, kdim, tn), lambda j, i: (layer, 0, j), pipeline_mode=pl.Buffered(1))


def proj(h, w3, widx, n, out_dtype, tn=1024, tm=1024):
    m, d = h.shape
    return pl.pallas_call(
        _proj_body,
        grid=(pl.cdiv(n, tn), m // tm),
        in_specs=[
            pl.BlockSpec((tm, d), lambda j, i: (i, 0)),
            _weight_spec(d, tn, widx),
        ],
        out_specs=pl.BlockSpec((tm, tn), lambda j, i: (i, j)),
        out_shape=jax.ShapeDtypeStruct((m, n), out_dtype),
        scratch_shapes=[pltpu.VMEM((d, tn), BF16)],
        compiler_params=_cparams(("parallel", "arbitrary")),
        name="proj",
    )(h, w3)


def _swiglu_body(h_ref, wg_ref, wu_ref, o_ref, wgb_ref, wub_ref):
    _cast_weight_once(wg_ref, wgb_ref)
    _cast_weight_once(wu_ref, wub_ref)
    h = h_ref[...]
    a = jnp.dot(h, wgb_ref[...], preferred_element_type=F32)
    b = jnp.dot(h, wub_ref[...], preferred_element_type=F32)
    o_ref[...] = (_silu(a) * b).astype(o_ref.dtype)


def swiglu(h, wg3, wu3, layer, tn=512, tm=1024):
    m, d = h.shape
    n = wg3.shape[2]
    wspec = _weight_spec(d, tn, layer)
    return pl.pallas_call(
        _swiglu_body,
        grid=(pl.cdiv(n, tn), m // tm),
        in_specs=[pl.BlockSpec((tm, d), lambda j, i: (i, 0)), wspec, wspec],
        out_specs=pl.BlockSpec((tm, tn), lambda j, i: (i, j)),
        out_shape=jax.ShapeDtypeStruct((m, n), BF16),
        scratch_shapes=[pltpu.VMEM((d, tn), BF16), pltpu.VMEM((d, tn), BF16)],
        compiler_params=_cparams(("parallel", "arbitrary")),
        name="swiglu",
    )(h, wg3, wu3)


def _mixer_out_body(a_ref, w_ref, g_ref, *refs, tm):
    *x_refs, o_ref, wb_ref = refs
    _cast_weight_once(w_ref, wb_ref)
    if len(x_refs) == 1:
        x = x_refs[0][...]
    else:
        x = jnp.where(pl.program_id(1) < N_PROMPT // tm, x_refs[0][...], x_refs[1][...])
    o_ref[...] = x + g_ref[...] * jnp.dot(a_ref[...], wb_ref[...], preferred_element_type=F32)


def mixer_out_proj(a, w3, widx, x_parts, mods, layer, tn=512, tm=1024):
    m, kdim = a.shape
    n = w3.shape[2]
    return pl.pallas_call(
        functools.partial(_mixer_out_body, tm=tm),
        grid=(n // tn, m // tm),
        in_specs=[
            pl.BlockSpec((tm, kdim), lambda j, i: (i, 0)),
            _weight_spec(kdim, tn, widx),
            pl.BlockSpec((None, None, None, 1, tn), lambda j, i: (layer, _cond_row(i, tm), 2, 0, j)),
            *_row_part_specs(x_parts, tm, tn, lambda j, i: (i, j)),
        ],
        out_specs=pl.BlockSpec((tm, tn), lambda j, i: (i, j)),
        out_shape=jax.ShapeDtypeStruct((m, n), F32),
        scratch_shapes=[pltpu.VMEM((kdim, tn), BF16)],
        compiler_params=_cparams(("parallel", "arbitrary")),
        name="mixer_out_proj",
    )(a, w3, mods, *x_parts)


def _ffn_down_body(a_ref, w_ref, x_ref, g_ref, o_ref):
    o_ref[...] = x_ref[...] + g_ref[...] * jnp.dot(a_ref[...], w_ref[...], preferred_element_type=F32)


def ffn_down_proj(a, w3, x, mods, layer, row0=0, nrows=None, tn=512, tm=512):
    kdim = a.shape[1]
    n = w3.shape[2]
    nrows = a.shape[0] if nrows is None else nrows
    i0 = row0 // tm
    return pl.pallas_call(
        _ffn_down_body,
        grid=(nrows // tm, n // tn),
        in_specs=[
            pl.BlockSpec((tm, kdim), lambda i, j: (i0 + i, 0)),
            pl.BlockSpec((None, kdim, tn), lambda i, j: (layer, 0, j)),
            pl.BlockSpec((tm, tn), lambda i, j: (i0 + i, j)),
            pl.BlockSpec((None, None, None, 1, tn), lambda i, j: (layer, _cond_row(i0 + i, tm), 5, 0, j)),
        ],
        out_specs=pl.BlockSpec((tm, tn), lambda i, j: (i, j)),
        out_shape=jax.ShapeDtypeStruct((nrows, n), F32),
        compiler_params=_cparams(("parallel", "parallel")),
        name="ffn_down_proj",
    )(a, w3, x, mods)


def _head_rms(x, w):
    ms = jnp.mean(x * x, axis=-1, keepdims=True)
    return x * lax.rsqrt(ms + EPS) * w


def _nt_dot(a, b):
    return lax.dot_general(a, b, (((1,), (1,)), ((), ())), preferred_element_type=F32)


def _dense_attn_body(q_ref, k_ref, v_ref, qw_ref, kw_ref, *rest, heads):
    o_ref, kn_ref, vo_ref = rest[-3:]
    scale = 1.0 / math.sqrt(NA_HEAD_DIM)
    vo_ref[...] = v_ref[...]
    for hh in range(heads):
        sl = slice(hh * NA_HEAD_DIM, (hh + 1) * NA_HEAD_DIM)
        qn = _head_rms(q_ref[:, sl], qw_ref[...]) * scale
        kn = _head_rms(k_ref[:, sl], kw_ref[...])
        kn_ref[:, sl] = kn
        s = _nt_dot(qn.astype(BF16), kn.astype(BF16))
        e = jnp.exp(s - jnp.max(s, axis=-1, keepdims=True))
        o = jnp.dot(e.astype(BF16), v_ref[:, sl].astype(BF16), preferred_element_type=F32)
        o_ref[:, sl] = (o / jnp.sum(e, axis=-1, keepdims=True)).astype(o_ref.dtype)


def dense_attention(proj, q_norm_w, k_norm_w, j, n_even, caches=None):
    heads = 4
    bw = heads * NA_HEAD_DIM
    nb = NA_WIDTH // bw
    cache_spec = pl.BlockSpec((None, None, SEQ, bw), lambda s, h: (s, j, 0, h))
    cache_shape = jax.ShapeDtypeStruct((BATCH, n_even, SEQ, NA_WIDTH), F32)
    carried = () if caches is None else tuple(caches)
    return pl.pallas_call(
        functools.partial(_dense_attn_body, heads=heads),
        grid=(BATCH, nb),
        in_specs=[
            pl.BlockSpec((SEQ, bw), lambda s, h: (s, h)),
            pl.BlockSpec((SEQ, bw), lambda s, h: (s, nb + h)),
            pl.BlockSpec((SEQ, bw), lambda s, h: (s, 2 * nb + h)),
            pl.BlockSpec((1, NA_HEAD_DIM), lambda s, h: (0, 0)),
            pl.BlockSpec((1, NA_HEAD_DIM), lambda s, h: (0, 0)),
            *[pl.BlockSpec(memory_space=pl.ANY) for _ in carried],
        ],
        out_specs=[pl.BlockSpec((SEQ, bw), lambda s, h: (s, h)), cache_spec, cache_spec],
        out_shape=[jax.ShapeDtypeStruct((N_TOK, D_MODEL), BF16), cache_shape, cache_shape],
        input_output_aliases={5 + n: 1 + n for n in range(len(carried))},
        compiler_params=_cparams(("parallel", "parallel")),
        name="dense_attention",
    )(proj, proj, proj, q_norm_w.reshape(1, -1), k_norm_w.reshape(1, -1), *carried)


def _na_window_start(blk):
    return jnp.clip(NA_QROWS * blk - NA_ROWS // 2, 0, GRID_ROWS - NA_KROWS)


def _na_attn_body(q_ref, k_ref, v_ref, ck_ref, cv_ref, bias_ref, qw_ref, kw_ref, mix_ref, o_ref,
                  kn_ref, vb_ref, ckb_ref, cvb_ref):
    del mix_ref
    scale = 1.0 / math.sqrt(NA_HEAD_DIM)
    blk = pl.program_id(2)

    heads = [slice(hh * NA_HEAD_DIM, (hh + 1) * NA_HEAD_DIM) for hh in range(NA_HEADS_PER_STEP)]

    @pl.when(blk == 0)
    def _():
        for sl in heads:
            kn_ref[:, sl] = _head_rms(k_ref[:, sl], kw_ref[...]).astype(BF16)
        vb_ref[...] = v_ref[...].astype(BF16)
        ckb_ref[...] = ck_ref[...].astype(BF16)
        cvb_ref[...] = cv_ref[...].astype(BF16)

    w0 = pl.multiple_of(_na_window_start(blk) * GRID_W, GRID_W)
    nk = NA_KROWS * GRID_W
    for hh, sl in enumerate(heads):
        qn = (_head_rms(q_ref[:, sl], qw_ref[...]) * scale).astype(BF16)
        s_loc = _nt_dot(qn, kn_ref[pl.ds(w0, nk), sl]) + bias_ref[hh]
        s_ctx = _nt_dot(qn, ckb_ref[:, sl])
        m = jnp.maximum(jnp.max(s_loc, axis=-1, keepdims=True), jnp.max(s_ctx, axis=-1, keepdims=True))
        e_loc = jnp.exp(s_loc - m)
        e_ctx = jnp.exp(s_ctx - m)
        den = jnp.sum(e_loc, axis=-1, keepdims=True) + jnp.sum(e_ctx, axis=-1, keepdims=True)
        o = (jnp.dot(e_loc.astype(BF16), vb_ref[pl.ds(w0, nk), sl], preferred_element_type=F32)
             + jnp.dot(e_ctx.astype(BF16), cvb_ref[:, sl], preferred_element_type=F32))
        o_ref[:, sl] = (o / den).astype(o_ref.dtype)


def _na_bias_table(rpb):
    nh, nblk, w = rpb.shape[0], GRID_ROWS // NA_QROWS, GRID_W
    v = jnp.pad(rpb, ((0, 0), (0, 0), (w - NA_COLS, w - NA_COLS + 1)))
    toe = jnp.tile(v, (1, 1, w))[:, :, :w * (2 * w - 1)].reshape(nh, -1, w, 2 * w - 1)[..., w - 1:]
    qc, kc = np.arange(w)[:, None], np.arange(w)[None, :]
    cs = np.clip(qc - NA_COLS // 2, 0, w - NA_COLS)
    toe = jnp.where(((kc >= cs) & (kc < cs + NA_COLS))[None, None], toe, NEG)
    masked = jnp.full((nh, w, w), NEG, F32)
    tabs = []
    for blk in (0, 1, nblk - 1):
        w0 = int(np.clip(NA_QROWS * blk - NA_ROWS // 2, 0, GRID_ROWS - NA_KROWS))
        qrows = []
        for a in range(NA_QROWS):
            qr = NA_QROWS * blk + a
            rs = int(np.clip(qr - NA_ROWS // 2, 0, GRID_ROWS - NA_ROWS))
            krows = [toe[:, w0 + t - qr + NA_ROWS - 1] if rs <= w0 + t < rs + NA_ROWS else masked
                     for t in range(NA_KROWS)]
            qrows.append(jnp.stack(krows, axis=2))
        tabs.append(jnp.stack(qrows, axis=1).reshape(nh, NA_QROWS * w, NA_KROWS * w))
    return jnp.stack(tabs, axis=1)


def neighbourhood_attention(proj, cache_k4, cache_v4, j, rpb, q_norm_w, k_norm_w, mix):
    nblk = GRID_ROWS // NA_QROWS
    tq = NA_QROWS * GRID_W
    tk = NA_KROWS * GRID_W
    bias = _na_bias_table(rpb)
    qrow0 = N_PROMPT // tq
    krow0 = N_PROMPT // DEC_SEQ

    def variant(blk):
        return jnp.where(blk == 0, 0, jnp.where(blk == nblk - 1, 2, 1))

    hd = NA_HEAD_DIM
    nhs = NA_HEADS_PER_STEP
    bw = nhs * hd
    ng = NA_HEADS // nhs
    return pl.pallas_call(
        _na_attn_body,
        grid=(DEC_BATCH, ng, nblk),
        in_specs=[
            pl.BlockSpec((tq, bw), lambda b, h, t: (qrow0 + b * nblk + t, h)),
            pl.BlockSpec((DEC_SEQ, bw), lambda b, h, t: (krow0 + b, ng + h)),
            pl.BlockSpec((DEC_SEQ, bw), lambda b, h, t: (krow0 + b, 2 * ng + h)),
            pl.BlockSpec((None, None, PAST_LEN, bw), lambda b, h, t: (b, j, 0, h)),
            pl.BlockSpec((None, None, PAST_LEN, bw), lambda b, h, t: (b, j, 0, h)),
            pl.BlockSpec((nhs, None, tq, tk), lambda b, h, t: (h, variant(t), 0, 0)),
            pl.BlockSpec((1, hd), lambda b, h, t: (0, 0)),
            pl.BlockSpec((1, hd), lambda b, h, t: (0, 0)),
            pl.BlockSpec(memory_space=pl.ANY),
        ],
        out_specs=pl.BlockSpec((tq, bw), lambda b, h, t: (qrow0 + b * nblk + t, h)),
        out_shape=jax.ShapeDtypeStruct(mix.shape, mix.dtype),
        input_output_aliases={8: 0},
        scratch_shapes=[pltpu.VMEM((DEC_SEQ, bw), BF16), pltpu.VMEM((DEC_SEQ, bw), BF16),
                        pltpu.VMEM((PAST_LEN, bw), BF16), pltpu.VMEM((PAST_LEN, bw), BF16)],
        compiler_params=_cparams(("parallel", "parallel", "arbitrary")),
        name="neighbourhood_attention",
    )(proj, proj, proj, cache_k4, cache_v4, bias, q_norm_w.reshape(1, -1), k_norm_w.reshape(1, -1), mix)


def _tile_seq_pos(c, rt):
    n_p = N_PROMPT // rt
    pp, ps = SEQ // rt, DEC_SEQ // rt
    pos = jnp.where(c < n_p, c % pp, (c - n_p) % ps)
    per = jnp.where(c < n_p, pp, ps)
    return pos == 0, pos == per - 1


def _halo_specs(rt, width, col_of, halo):
    per = rt // halo
    last = N_TOK // halo - 1
    return [
        pl.BlockSpec((halo, width), lambda c, j: (jnp.maximum(c * per - 1, 0), col_of(j))),
        pl.BlockSpec((rt, width), lambda c, j: (c, col_of(j))),
        pl.BlockSpec((halo, width), lambda c, j: (jnp.minimum((c + 1) * per, last), col_of(j))),
    ]


def _ssd_conv_body(prev_ref, cur_ref, next_ref, w_ref, b_ref, o_ref, ext_ref, *, rt):
    first, last = _tile_seq_pos(pl.program_id(0), rt)
    ext_ref[0:HALO] = jnp.where(first, 0.0, prev_ref[...])
    ext_ref[HALO:HALO + rt] = cur_ref[...]
    ext_ref[HALO + rt:] = jnp.where(last, 0.0, next_ref[...])
    pad = SSD_CONV_W // 2
    acc = b_ref[...] + ext_ref[HALO - pad:HALO - pad + rt] * w_ref[0:1]
    for k in range(1, SSD_CONV_W):
        acc = acc + ext_ref[HALO - pad + k:HALO - pad + k + rt] * w_ref[k:k + 1]
    o_ref[...] = _silu(acc)


def ssd_conv(proj, conv_w, conv_b, rt=ROW_TILE, width=1024):
    col0 = (3 * NA_WIDTH + SSD_INNER) // width
    return pl.pallas_call(
        functools.partial(_ssd_conv_body, rt=rt),
        grid=(N_TOK // rt, SSD_CONV_DIM // width),
        in_specs=[
            *_halo_specs(rt, width, lambda j: col0 + j, HALO),
            pl.BlockSpec((SSD_CONV_W, width), lambda c, j: (0, j)),
            pl.BlockSpec((1, width), lambda c, j: (0, j)),
        ],
        out_specs=pl.BlockSpec((rt, width), lambda c, j: (c, j)),
        out_shape=jax.ShapeDtypeStruct((N_TOK, SSD_CONV_DIM), F32),
        scratch_shapes=[pltpu.VMEM((rt + 2 * HALO, width), F32)],
        compiler_params=_cparams(("parallel", "parallel")),
        name="ssd_conv",
    )(proj, proj, proj, conv_w, conv_b.reshape(1, -1))


def _short_conv_body(gb_ref, cp_ref, cc_ref, cn_ref, xp_ref, xc_ref, xn_ref, w_ref, o_ref, ext_ref, *, rt):
    first, last = _tile_seq_pos(pl.program_id(0), rt)

    def prod(c_ref, x_ref):
        return c_ref[...].astype(F32) * x_ref[...].astype(F32)

    h = HALO_BF16
    ext_ref[0:h] = jnp.where(first, 0.0, prod(cp_ref, xp_ref))
    ext_ref[h:h + rt] = prod(cc_ref, xc_ref)
    ext_ref[h + rt:] = jnp.where(last, 0.0, prod(cn_ref, xn_ref))
    pad = SC_CONV_W // 2
    acc = ext_ref[h - pad:h - pad + rt] * w_ref[0:1]
    for k in range(1, SC_CONV_W):
        acc = acc + ext_ref[h - pad + k:h - pad + k + rt] * w_ref[k:k + 1]
    o_ref[...] = (gb_ref[...].astype(F32) * acc).astype(o_ref.dtype)


def short_conv_gate(proj, conv_w, rt=ROW_TILE, width=1024):
    nb = SC_WIDTH // width
    return pl.pallas_call(
        functools.partial(_short_conv_body, rt=rt),
        grid=(N_TOK // rt, nb),
        in_specs=[
            pl.BlockSpec((rt, width), lambda c, j: (c, j)),
            *_halo_specs(rt, width, lambda j: nb + j, HALO_BF16),
            *_halo_specs(rt, width, lambda j: 2 * nb + j, HALO_BF16),
            pl.BlockSpec((SC_CONV_W, width), lambda c, j: (0, j)),
        ],
        out_specs=pl.BlockSpec((rt, width), lambda c, j: (c, j)),
        out_shape=jax.ShapeDtypeStruct((N_TOK, SC_WIDTH), BF16),
        scratch_shapes=[pltpu.VMEM((rt + 2 * HALO_BF16, width), F32)],
        compiler_params=_cparams(("parallel", "parallel")),
        name="short_conv_gate",
    )(proj, proj, proj, proj, proj, proj, proj, conv_w)


def _chunk_cumsum(x, reverse):
    rows = lax.broadcasted_iota(jnp.int32, x.shape, 0)
    n = x.shape[0]
    k = 1
    while k < n:
        if reverse:
            x = x + jnp.where(rows < n - k, pltpu.roll(x, n - k, 0), 0.0)
        else:
            x = x + jnp.where(rows >= k, pltpu.roll(x, k, 0), 0.0)
        k *= 2
    return x


def _dt_and_logdecay(dt_ref, bias_ref, a_ref):
    dtv = _softplus(dt_ref[...] + bias_ref[...])
    dta = dtv * a_ref[...]
    lane = lax.broadcasted_iota(jnp.int32, dta.shape, 1)
    acum = jnp.where(lane < SSD_HEADS, _chunk_cumsum(dta, False), _chunk_cumsum(dta, True))
    return dtv, acum


def _pair_cols(m, c0, lane):
    shape = (m.shape[0], LANES)
    return jnp.where(lane[:m.shape[0]] < SSD_HEAD_DIM,
                     jnp.broadcast_to(m[:, c0:c0 + 1], shape),
                     jnp.broadcast_to(m[:, c0 + 1:c0 + 2], shape))


def _ssd_state_body(xf_ref, bf_ref, dtf_ref, xb_ref, bb_ref, dtb_ref, h0_ref, bias_ref, a_ref,
                    hpf_ref, hpb_ref, hfin_ref, h_ref, *, nc):
    t = pl.program_id(1)

    @pl.when(t == 0)
    def _():
        h_ref[...] = h0_ref[...]

    lane = lax.broadcasted_iota(jnp.int32, (SSD_CHUNK, LANES), 1)
    for d, (x_ref, b_ref, dt_ref, hp_ref) in enumerate(
            ((xf_ref, bf_ref, dtf_ref, hpf_ref), (xb_ref, bb_ref, dtb_ref, hpb_ref))):
        dtv, acum = _dt_and_logdecay(dt_ref, bias_ref, a_ref)
        edge = acum[SSD_CHUNK - 1:SSD_CHUNK] if d == 0 else acum[0:1]
        s = dtv * jnp.exp(edge - acum)
        cd = jnp.exp(edge)
        hp_ref[...] = h_ref[d].astype(BF16)
        for g in range(SSD_GROUPS):
            bt = jnp.transpose(b_ref[:, g * SSD_STATE:(g + 1) * SSD_STATE]).astype(BF16)
            for pr in range(SSD_HPG // 2):
                head = g * SSD_HPG + 2 * pr
                c0 = d * SSD_HEADS + head
                cols = slice(head * SSD_HEAD_DIM, head * SSD_HEAD_DIM + LANES)
                xd = (x_ref[:, cols] * _pair_cols(s, c0, lane)).astype(BF16)
                st = jnp.dot(bt, xd, preferred_element_type=F32)
                h_ref[d, :, cols] = h_ref[d, :, cols] * _pair_cols(cd, c0, lane) + st

    @pl.when(t == nc - 1)
    def _():
        hfin_ref[...] = h_ref[...]


def ssd_states(xbc, dt_raw, h0t, dt_bias, a_neg, row0, nseq, nc):
    c0 = row0 // SSD_CHUNK
    hp = SSD_INNER
    bcol = SSD_INNER // (SSD_GROUPS * SSD_STATE)

    def fwd(s, t):
        return c0 + s * nc + t

    def bwd(s, t):
        return c0 + s * nc + (nc - 1 - t)

    return pl.pallas_call(
        functools.partial(_ssd_state_body, nc=nc),
        grid=(nseq, nc),
        in_specs=[
            pl.BlockSpec((SSD_CHUNK, hp), lambda s, t: (fwd(s, t), 0)),
            pl.BlockSpec((SSD_CHUNK, SSD_GROUPS * SSD_STATE), lambda s, t: (fwd(s, t), bcol)),
            pl.BlockSpec((SSD_CHUNK, LANES), lambda s, t: (fwd(s, t), 0)),
            pl.BlockSpec((SSD_CHUNK, hp), lambda s, t: (bwd(s, t), 0)),
            pl.BlockSpec((SSD_CHUNK, SSD_GROUPS * SSD_STATE), lambda s, t: (bwd(s, t), bcol)),
            pl.BlockSpec((SSD_CHUNK, LANES), lambda s, t: (bwd(s, t), 0)),
            pl.BlockSpec((None, 2, SSD_STATE, hp), lambda s, t: (s, 0, 0, 0)),
            pl.BlockSpec((1, LANES), lambda s, t: (0, 0)),
            pl.BlockSpec((1, LANES), lambda s, t: (0, 0)),
        ],
        out_specs=[
            pl.BlockSpec((None, None, SSD_STATE, hp), lambda s, t: (s, t, 0, 0)),
            pl.BlockSpec((None, None, SSD_STATE, hp), lambda s, t: (s, nc - 1 - t, 0, 0)),
            pl.BlockSpec((None, 2, SSD_STATE, hp), lambda s, t: (s, 0, 0, 0)),
        ],
        out_shape=[
            jax.ShapeDtypeStruct((nseq, nc, SSD_STATE, hp), BF16),
            jax.ShapeDtypeStruct((nseq, nc, SSD_STATE, hp), BF16),
            jax.ShapeDtypeStruct((nseq, 2, SSD_STATE, hp), F32),
        ],
        scratch_shapes=[pltpu.VMEM((2, SSD_STATE, hp), F32)],
        compiler_params=_cparams(("parallel", "arbitrary")),
        name="ssd_states",
    )(xbc, xbc, dt_raw, xbc, xbc, dt_raw, h0t, dt_bias, a_neg)


def _ssd_out_body(x_ref, b_ref, c_ref, z_ref, dt_ref, hpf_ref, hpb_ref, bias_ref, a_ref, dsk_ref, nw_ref,
                  mix_ref, o_ref, g_ref):
    del mix_ref
    dtv, acum = _dt_and_logdecay(dt_ref, bias_ref, a_ref)
    acum_t = jnp.transpose(acum)
    dt_t = jnp.transpose(dtv)
    shape = (SSD_CHUNK, SSD_CHUNK)
    rows = lax.broadcasted_iota(jnp.int32, shape, 0)
    lane = lax.broadcasted_iota(jnp.int32, shape, 1)
    lower = rows >= lane
    upper = rows <= lane

    def head_terms(cb, head):
        cf, cbk = head, SSD_HEADS + head
        col_f = jnp.broadcast_to(acum[:, cf:cf + 1], shape)
        col_b = jnp.broadcast_to(acum[:, cbk:cbk + 1], shape)
        lf = jnp.exp(jnp.where(lower, col_f - acum_t[cf:cf + 1, :], -jnp.inf)) * dt_t[cf:cf + 1, :]
        lb = jnp.exp(jnp.where(upper, col_b - acum_t[cbk:cbk + 1, :], -jnp.inf)) * dt_t[cbk:cbk + 1, :]
        return (cb * (lf + lb)).astype(BF16), jnp.exp(col_f), jnp.exp(col_b)

    for g in range(SSD_GROUPS):
        gcols = slice(g * SSD_GROUP_W, (g + 1) * SSD_GROUP_W)
        bg = b_ref[:, g * SSD_STATE:(g + 1) * SSD_STATE].astype(BF16)
        cg = c_ref[:, g * SSD_STATE:(g + 1) * SSD_STATE].astype(BF16)
        cb = _nt_dot(cg, bg)
        yoff_f = jnp.dot(cg, hpf_ref[:, gcols], preferred_element_type=F32)
        yoff_b = jnp.dot(cg, hpb_ref[:, gcols], preferred_element_type=F32)
        for pr in range(SSD_HPG // 2):
            head = g * SSD_HPG + 2 * pr
            cols = slice(head * SSD_HEAD_DIM, head * SSD_HEAD_DIM + LANES)
            pcols = slice(pr * LANES, (pr + 1) * LANES)
            xp = x_ref[:, cols]
            xb = xp.astype(BF16)
            first = lane < SSD_HEAD_DIM
            w0, ef0, eb0 = head_terms(cb, head)
            w1, ef1, eb1 = head_terms(cb, head + 1)
            y = jnp.where(first, jnp.dot(w0, xb, preferred_element_type=F32),
                          jnp.dot(w1, xb, preferred_element_type=F32))
            y = y + jnp.where(first, ef0, ef1) * yoff_f[:, pcols]
            y = y + jnp.where(first, eb0, eb1) * yoff_b[:, pcols]
            y = y + xp * dsk_ref[:, cols]
            g_ref[:, cols] = y * _silu(z_ref[:, cols])
        gg = g_ref[:, gcols]
        ms = jnp.mean(gg * gg, axis=-1, keepdims=True)
        o_ref[:, gcols] = (gg * lax.rsqrt(ms + EPS) * nw_ref[:, gcols]).astype(o_ref.dtype)


def ssd_outputs(xbc, proj, dt_raw, hpf, hpb, dt_bias, a_neg, d_skip, norm_w, row0, nseq, nc, mix):
    c0 = row0 // SSD_CHUNK
    hp = SSD_INNER
    gw = SSD_GROUPS * SSD_STATE
    bcol = SSD_INNER // gw
    zcol = 3 * NA_WIDTH // SSD_INNER

    def row(s, t):
        return c0 + s * nc + t

    return pl.pallas_call(
        _ssd_out_body,
        grid=(nseq, nc),
        in_specs=[
            pl.BlockSpec((SSD_CHUNK, hp), lambda s, t: (row(s, t), 0)),
            pl.BlockSpec((SSD_CHUNK, gw), lambda s, t: (row(s, t), bcol)),
            pl.BlockSpec((SSD_CHUNK, gw), lambda s, t: (row(s, t), bcol + 1)),
            pl.BlockSpec((SSD_CHUNK, hp), lambda s, t: (row(s, t), zcol)),
            pl.BlockSpec((SSD_CHUNK, LANES), lambda s, t: (row(s, t), 0)),
            pl.BlockSpec((None, None, SSD_STATE, hp), lambda s, t: (s, t, 0, 0)),
            pl.BlockSpec((None, None, SSD_STATE, hp), lambda s, t: (s, t, 0, 0)),
            pl.BlockSpec((1, LANES), lambda s, t: (0, 0)),
            pl.BlockSpec((1, LANES), lambda s, t: (0, 0)),
            pl.BlockSpec((1, hp), lambda s, t: (0, 0)),
            pl.BlockSpec((1, hp), lambda s, t: (0, 0)),
            pl.BlockSpec(memory_space=pl.ANY),
        ],
        out_specs=pl.BlockSpec((SSD_CHUNK, hp), lambda s, t: (row(s, t), NA_WIDTH // hp)),
        out_shape=jax.ShapeDtypeStruct(mix.shape, mix.dtype),
        input_output_aliases={11: 0},
        scratch_shapes=[pltpu.VMEM((SSD_CHUNK, hp), F32)],
        compiler_params=_cparams(("parallel", "parallel")),
        name="ssd_outputs",
    )(xbc, xbc, xbc, proj, dt_raw, hpf, hpb, dt_bias, a_neg, d_skip, norm_w, mix)


def _pad_lanes(v):
    return jnp.pad(v.reshape(1, -1), ((0, 0), (0, LANES - v.size)))


def ssd_mixer(proj, dt_raw, state_h0, conv_w, conv_b, a_log, dt_bias, d_skip, norm_w, mix):
    xbc = ssd_conv(proj, conv_w, conv_b)
    bias = _pad_lanes(dt_bias)
    a_neg = _pad_lanes(-jnp.exp(a_log))
    dsk = jnp.repeat(d_skip, SSD_HEAD_DIM).reshape(1, -1)
    nw = norm_w.reshape(1, -1)

    def to_t(h):
        return jnp.transpose(h, (0, 1, 4, 2, 3)).reshape(h.shape[0], 2, SSD_STATE, SSD_INNER)

    fins = []
    for row0, nseq, nc, h0 in ((0, BATCH, SEQ // SSD_CHUNK, None),
                               (N_PROMPT, DEC_BATCH, DEC_SEQ // SSD_CHUNK, state_h0)):
        h0t = jnp.zeros((nseq, 2, SSD_STATE, SSD_INNER), F32) if h0 is None else to_t(h0)
        hpf, hpb, hfin = ssd_states(xbc, dt_raw, h0t, bias, a_neg, row0, nseq, nc)
        mix = ssd_outputs(xbc, proj, dt_raw, hpf, hpb, bias, a_neg, dsk, nw, row0, nseq, nc, mix)
        fins.append(hfin)
    hfin_p = fins[0].reshape(BATCH, 2, SSD_STATE, SSD_HEADS, SSD_HEAD_DIM)
    return mix, jnp.transpose(hfin_p, (0, 1, 3, 4, 2))


def kernel(x_prompt, x_sample, cache_k, cache_v, state_ssm, c, c_ctx, norm1_w, norm2_w, w_mod, b_mod,
           w_in_even, w_out_even, q_norm_w, k_norm_w, na_rel_bias, ssd_conv_w, ssd_conv_b, ssd_a_log,
           ssd_dt_bias, ssd_d, ssd_norm_w, w_in_odd, sc_conv_w, w_out_odd, w_ffn_gate, w_ffn_up,
           w_ffn_down):
    d = D_MODEL
    x = (x_prompt.reshape(N_PROMPT, d), x_sample.reshape(N_SAMPLE, d))
    cond8 = jnp.concatenate([c_ctx[None, :], c, jnp.zeros((8 - 1 - DEC_BATCH, d), F32)], axis=0)
    mods = adaln_table(cond8, w_mod, b_mod)
    n_even = w_in_even.shape[0]
    cache_k4 = cache_k.reshape(DEC_BATCH, n_even, PAST_LEN, NA_WIDTH)
    cache_v4 = cache_v.reshape(DEC_BATCH, n_even, PAST_LEN, NA_WIDTH)

    w_down = w_ffn_down.astype(BF16)
    caches, new_h = None, []
    for l in range(DEPTH):
        j = l // 2
        h = modnorm(x, norm1_w[l], mods, l, 0)
        if l % 2 == 0:
            w_dt = jnp.pad(w_in_even[j:j + 1, :, EVEN_MAIN:], ((0, 0), (0, 0), (0, LANES - 2 * SSD_HEADS)))
            pr = proj(h, w_in_even, j, EVEN_MAIN, F32)
            dt_raw = proj(h, w_dt, 0, LANES, F32, tn=LANES)
            mix, *caches = dense_attention(pr, q_norm_w[j], k_norm_w[j], j, n_even, caches)
            mix = neighbourhood_attention(pr, cache_k4, cache_v4, j, na_rel_bias[j],
                                          q_norm_w[j], k_norm_w[j], mix)
            mix, h_j = ssd_mixer(pr, dt_raw, state_ssm[:, j], ssd_conv_w[j], ssd_conv_b[j],
                                 ssd_a_log[j].reshape(-1), ssd_dt_bias[j].reshape(-1), ssd_d[j],
                                 ssd_norm_w[j], mix)
            x_mid = mixer_out_proj(mix, w_out_even, j, x, mods, l)
            new_h.append(h_j)
        else:
            pr = proj(h, w_in_odd, j, 3 * SC_WIDTH, BF16)
            mix = short_conv_gate(pr, sc_conv_w[j])
            x_mid = mixer_out_proj(mix, w_out_odd, j, x, mods, l)
        h = modnorm((x_mid,), norm2_w[l], mods, l, 3)
        hidden = swiglu(h, w_ffn_gate, w_ffn_up, l)
        if l < DEPTH - 1:
            x = (ffn_down_proj(hidden, w_down, x_mid, mods, l),)
        else:
            y_prompt = ffn_down_proj(hidden, w_down, x_mid, mods, l, 0, N_PROMPT).reshape(BATCH, SEQ, d)
            y_sample = ffn_down_proj(hidden, w_down, x_mid, mods, l, N_PROMPT, N_SAMPLE).reshape(DEC_BATCH, DEC_SEQ, d)

    new_k, new_v = (t.reshape(BATCH, n_even, SEQ, NA_HEADS, NA_HEAD_DIM) for t in caches)
    return (y_prompt, y_sample, new_k, new_v, jnp.stack(new_h, axis=1))
```

```python
import functools
import math

import numpy as np
import jax
import jax.numpy as jnp
from jax import lax
from jax.experimental import pallas as pl
from jax.experimental.pallas import tpu as pltpu

D_MODEL = 4096
BATCH, SEQ = 32, 256
DEC_BATCH, DEC_SEQ = 4, 4096
PAST_LEN = 256
DEPTH = 4
GRID_W = 64
GRID_ROWS = DEC_SEQ // GRID_W
NA_HEADS, NA_HEAD_DIM = 16, 128
NA_WIDTH = NA_HEADS * NA_HEAD_DIM
NA_ROWS, NA_COLS = 8, 16
SSD_HEAD_DIM = 64
SSD_INNER = D_MODEL // 2
SSD_HEADS = SSD_INNER // SSD_HEAD_DIM
SSD_GROUPS = 4
SSD_HPG = SSD_HEADS // SSD_GROUPS
SSD_STATE = 128
SSD_CONV_W = 5
SSD_CHUNK = 128
SSD_CONV_DIM = SSD_INNER + 2 * SSD_GROUPS * SSD_STATE
SSD_GROUP_W = SSD_HPG * SSD_HEAD_DIM
EVEN_MAIN = 3 * NA_WIDTH + SSD_INNER + SSD_CONV_DIM
SC_WIDTH = D_MODEL
SC_CONV_W = 3
EPS = 1e-6

N_PROMPT = BATCH * SEQ
N_SAMPLE = DEC_BATCH * DEC_SEQ
N_TOK = N_PROMPT + N_SAMPLE

LANES = 128
SUBLANES = 8
VMEM_LIMIT = 56 * 1024 * 1024
ROW_TILE = 256
HALO = SUBLANES
HALO_BF16 = 2 * SUBLANES
NA_QROWS = 4
NA_KROWS = NA_QROWS + NA_ROWS
NA_HEADS_PER_STEP = 2
NEG = -1e30

F32 = jnp.float32
BF16 = jnp.bfloat16


def _cparams(sem):
    return pltpu.CompilerParams(dimension_semantics=sem, vmem_limit_bytes=VMEM_LIMIT)


def _silu(x):
    return x / (1.0 + jnp.exp(-x))


def _softplus(x):
    return jnp.maximum(x, 0.0) + jnp.log(1.0 + jnp.exp(-jnp.abs(x)))


def _cond_row(i, tm):
    n_p = N_PROMPT // tm
    per = DEC_SEQ // tm
    return jnp.where(i < n_p, 0, 1 + (i - n_p) // per)


def _adaln_body(c_ref, w_ref, b_ref, o_ref):
    a = _silu(c_ref[...]).astype(BF16)
    o_ref[...] = jnp.dot(a, w_ref[...].astype(BF16), preferred_element_type=F32) + b_ref[...]


def adaln_table(cond8, w_mod, b_mod):
    depth, d, n6 = w_mod.shape
    tn = 512
    out = pl.pallas_call(
        _adaln_body,
        grid=(depth, n6 // tn),
        in_specs=[
            pl.BlockSpec((8, d), lambda l, j: (0, 0)),
            pl.BlockSpec((None, d, tn), lambda l, j: (l, 0, j)),
            pl.BlockSpec((None, 1, tn), lambda l, j: (l, 0, j)),
        ],
        out_specs=pl.BlockSpec((None, 8, tn), lambda l, j: (l, 0, j)),
        out_shape=jax.ShapeDtypeStruct((depth, 8, n6), F32),
        compiler_params=_cparams(("parallel", "parallel")),
        name="adaln_table",
    )(cond8, w_mod, b_mod.reshape(depth, 1, n6))
    return out.reshape(depth, 8, 6, 1, d)


def _row_part_specs(parts, tm, tn, ids):
    if len(parts) == 1:
        return [pl.BlockSpec((tm, tn), lambda *g: ids(*g))]
    n_p = N_PROMPT // tm
    return [
        pl.BlockSpec((tm, tn), lambda *g: (jnp.minimum(ids(*g)[0], n_p - 1), ids(*g)[1])),
        pl.BlockSpec((tm, tn), lambda *g: (jnp.maximum(ids(*g)[0] - n_p, 0), ids(*g)[1])),
    ]


def _modnorm_body(*refs, tm):
    *x_refs, g_ref, sh_ref, sc_ref, h_ref, gain_ref = refs
    gain_ref[...] = g_ref[...] * (1.0 + sc_ref[...])
    d = h_ref.shape[1]
    rows = 8 * SUBLANES
    chunk = 2 * LANES

    def run(x_ref):
        def body(r, carry):
            sl = pl.ds(pl.multiple_of(r * rows, rows), rows)
            acc = jnp.zeros((rows, LANES), F32)
            for c in range(0, d, LANES):
                xc = x_ref[sl, c:c + LANES]
                acc = acc + xc * xc
            rinv = lax.rsqrt(jnp.sum(acc, axis=-1, keepdims=True) / d + EPS)
            for c in range(0, d, chunk):
                cs = slice(c, c + chunk)
                h_ref[sl, cs] = (x_ref[sl, cs] * rinv * gain_ref[:, cs] + sh_ref[:, cs]).astype(h_ref.dtype)
            return carry

        lax.fori_loop(0, tm // rows, body, 0)

    if len(x_refs) == 1:
        run(x_refs[0])
    else:
        in_prompt = pl.program_id(0) < N_PROMPT // tm
        pl.when(in_prompt)(lambda: run(x_refs[0]))
        pl.when(jnp.logical_not(in_prompt))(lambda: run(x_refs[1]))


def modnorm(x_parts, norm_w, mods, layer, which_shift, tm=2 * ROW_TILE):
    d = D_MODEL

    def mod_spec(which):
        return pl.BlockSpec((None, None, None, 1, d), lambda i: (layer, _cond_row(i, tm), which, 0, 0))

    return pl.pallas_call(
        functools.partial(_modnorm_body, tm=tm),
        grid=(N_TOK // tm,),
        in_specs=[
            *_row_part_specs(x_parts, tm, d, lambda i: (i, 0)),
            pl.BlockSpec((1, d), lambda i: (0, 0)),
            mod_spec(which_shift),
            mod_spec(which_shift + 1),
        ],
        out_specs=pl.BlockSpec((tm, d), lambda i: (i, 0)),
        out_shape=jax.ShapeDtypeStruct((N_TOK, d), BF16),
        scratch_shapes=[pltpu.VMEM((1, d), F32)],
        compiler_params=_cparams(("parallel",)),
        name="modnorm",
    )(*x_parts, norm_w.reshape(1, d), mods, mods)


def _cast_weight_once(w_ref, wb_ref):
    @pl.when(pl.program_id(1) == 0)
    def _():
        wb_ref[...] = w_ref[...].astype(BF16)


def _proj_body(h_ref, w_ref, o_ref, wb_ref):
    _cast_weight_once(w_ref, wb_ref)
    o_ref[...] = jnp.dot(h_ref[...], wb_ref[...], preferred_element_type=F32).astype(o_ref.dtype)


def _weight_spec(kdim, tn, layer):
    return pl.BlockSpec((None, kdim, tn), lambda j, i: (layer, 0, j), pipeline_mode=pl.Buffered(1))


def proj(h, w3, widx, n, out_dtype, tn=1024, tm=1024):
    m, d = h.shape
    return pl.pallas_call(
        _proj_body,
        grid=(pl.cdiv(n, tn), m // tm),
        in_specs=[
            pl.BlockSpec((tm, d), lambda j, i: (i, 0)),
            _weight_spec(d, tn, widx),
        ],
        out_specs=pl.BlockSpec((tm, tn), lambda j, i: (i, j)),
        out_shape=jax.ShapeDtypeStruct((m, n), out_dtype),
        scratch_shapes=[pltpu.VMEM((d, tn), BF16)],
        compiler_params=_cparams(("parallel", "arbitrary")),
        name="proj",
    )(h, w3)


def _swiglu_body(h_ref, wg_ref, wu_ref, o_ref, wgb_ref, wub_ref):
    _cast_weight_once(wg_ref, wgb_ref)
    _cast_weight_once(wu_ref, wub_ref)
    h = h_ref[...]
    a = jnp.dot(h, wgb_ref[...], preferred_element_type=F32)
    b = jnp.dot(h, wub_ref[...], preferred_element_type=F32)
    o_ref[...] = (_silu(a) * b).astype(o_ref.dtype)


def swiglu(h, wg3, wu3, layer, tn=512, tm=1024):
    m, d = h.shape
    n = wg3.shape[2]
    wspec = _weight_spec(d, tn, layer)
    return pl.pallas_call(
        _swiglu_body,
        grid=(pl.cdiv(n, tn), m // tm),
        in_specs=[pl.BlockSpec((tm, d), lambda j, i: (i, 0)), wspec, wspec],
        out_specs=pl.BlockSpec((tm, tn), lambda j, i: (i, j)),
        out_shape=jax.ShapeDtypeStruct((m, n), BF16),
        scratch_shapes=[pltpu.VMEM((d, tn), BF16), pltpu.VMEM((d, tn), BF16)],
        compiler_params=_cparams(("parallel", "arbitrary")),
        name="swiglu",
    )(h, wg3, wu3)


def _mixer_out_body(a_ref, w_ref, g_ref, *refs, tm):
    *x_refs, o_ref, wb_ref = refs
    _cast_weight_once(w_ref, wb_ref)
    if len(x_refs) == 1:
        x = x_refs[0][...]
    else:
        x = jnp.where(pl.program_id(1) < N_PROMPT // tm, x_refs[0][...], x_refs[1][...])
    o_ref[...] = x + g_ref[...] * jnp.dot(a_ref[...], wb_ref[...], preferred_element_type=F32)


def mixer_out_proj(a, w3, widx, x_parts, mods, layer, tn=512, tm=1024):
    m, kdim = a.shape
    n = w3.shape[2]
    return pl.pallas_call(
        functools.partial(_mixer_out_body, tm=tm),
        grid=(n // tn, m // tm),
        in_specs=[
            pl.BlockSpec((tm, kdim), lambda j, i: (i, 0)),
            _weight_spec(kdim, tn, widx),
            pl.BlockSpec((None, None, None, 1, tn), lambda j, i: (layer, _cond_row(i, tm), 2, 0, j)),
            *_row_part_specs(x_parts, tm, tn, lambda j, i: (i, j)),
        ],
        out_specs=pl.BlockSpec((tm, tn), lambda j, i: (i, j)),
        out_shape=jax.ShapeDtypeStruct((m, n), F32),
        scratch_shapes=[pltpu.VMEM((kdim, tn), BF16)],
        compiler_params=_cparams(("parallel", "arbitrary")),
        name="mixer_out_proj",
    )(a, w3, mods, *x_parts)


def _ffn_down_body(a_ref, w_ref, x_ref, g_ref, o_ref):
    o_ref[...] = x_ref[...] + g_ref[...] * jnp.dot(a_ref[...], w_ref[...], preferred_element_type=F32)


def ffn_down_proj(a, w3, x, mods, layer, row0=0, nrows=None, tn=512, tm=512):
    kdim = a.shape[1]
    n = w3.shape[2]
    nrows = a.shape[0] if nrows is None else nrows
    i0 = row0 // tm
    return pl.pallas_call(
        _ffn_down_body,
        grid=(nrows // tm, n // tn),
        in_specs=[
            pl.BlockSpec((tm, kdim), lambda i, j: (i0 + i, 0)),
            pl.BlockSpec((None, kdim, tn), lambda i, j: (layer, 0, j)),
            pl.BlockSpec((tm, tn), lambda i, j: (i0 + i, j)),
            pl.BlockSpec((None, None, None, 1, tn), lambda i, j: (layer, _cond_row(i0 + i, tm), 5, 0, j)),
        ],
        out_specs=pl.BlockSpec((tm, tn), lambda i, j: (i, j)),
        out_shape=jax.ShapeDtypeStruct((nrows, n), F32),
        compiler_params=_cparams(("parallel", "parallel")),
        name="ffn_down_proj",
    )(a, w3, x, mods)


def _head_rms(x, w):
    ms = jnp.mean(x * x, axis=-1, keepdims=True)
    return x * lax.rsqrt(ms + EPS) * w


def _nt_dot(a, b):
    return lax.dot_general(a, b, (((1,), (1,)), ((), ())), preferred_element_type=F32)


def _dense_attn_body(q_ref, k_ref, v_ref, qw_ref, kw_ref, *rest, heads):
    o_ref, kn_ref, vo_ref = rest[-3:]
    scale = 1.0 / math.sqrt(NA_HEAD_DIM)
    vo_ref[...] = v_ref[...]
    for hh in range(heads):
        sl = slice(hh * NA_HEAD_DIM, (hh + 1) * NA_HEAD_DIM)
        qn = _head_rms(q_ref[:, sl], qw_ref[...]) * scale
        kn = _head_rms(k_ref[:, sl], kw_ref[...])
        kn_ref[:, sl] = kn
        s = _nt_dot(qn.astype(BF16), kn.astype(BF16))
        e = jnp.exp(s - jnp.max(s, axis=-1, keepdims=True))
        o = jnp.dot(e.astype(BF16), v_ref[:, sl].astype(BF16), preferred_element_type=F32)
        o_ref[:, sl] = (o / jnp.sum(e, axis=-1, keepdims=True)).astype(o_ref.dtype)


def dense_attention(proj, q_norm_w, k_norm_w, j, n_even, caches=None):
    heads = 4
    bw = heads * NA_HEAD_DIM
    nb = NA_WIDTH // bw
    cache_spec = pl.BlockSpec((None, None, SEQ, bw), lambda s, h: (s, j, 0, h))
    cache_shape = jax.ShapeDtypeStruct((BATCH, n_even, SEQ, NA_WIDTH), F32)
    carried = () if caches is None else tuple(caches)
    return pl.pallas_call(
        functools.partial(_dense_attn_body, heads=heads),
        grid=(BATCH, nb),
        in_specs=[
            pl.BlockSpec((SEQ, bw), lambda s, h: (s, h)),
            pl.BlockSpec((SEQ, bw), lambda s, h: (s, nb + h)),
            pl.BlockSpec((SEQ, bw), lambda s, h: (s, 2 * nb + h)),
            pl.BlockSpec((1, NA_HEAD_DIM), lambda s, h: (0, 0)),
            pl.BlockSpec((1, NA_HEAD_DIM), lambda s, h: (0, 0)),
            *[pl.BlockSpec(memory_space=pl.ANY) for _ in carried],
        ],
        out_specs=[pl.BlockSpec((SEQ, bw), lambda s, h: (s, h)), cache_spec, cache_spec],
        out_shape=[jax.ShapeDtypeStruct((N_TOK, D_MODEL), BF16), cache_shape, cache_shape],
        input_output_aliases={5 + n: 1 + n for n in range(len(carried))},
        compiler_params=_cparams(("parallel", "parallel")),
        name="dense_attention",
    )(proj, proj, proj, q_norm_w.reshape(1, -1), k_norm_w.reshape(1, -1), *carried)


def _na_window_start(blk):
    return jnp.clip(NA_QROWS * blk - NA_ROWS // 2, 0, GRID_ROWS - NA_KROWS)


def _na_attn_body(q_ref, k_ref, v_ref, ck_ref, cv_ref, bias_ref, qw_ref, kw_ref, mix_ref, o_ref,
                  kn_ref, vb_ref, ckb_ref, cvb_ref):
    del mix_ref
    scale = 1.0 / math.sqrt(NA_HEAD_DIM)
    blk = pl.program_id(2)

    heads = [slice(hh * NA_HEAD_DIM, (hh + 1) * NA_HEAD_DIM) for hh in range(NA_HEADS_PER_STEP)]

    @pl.when(blk == 0)
    def _():
        for sl in heads:
            kn_ref[:, sl] = _head_rms(k_ref[:, sl], kw_ref[...]).astype(BF16)
        vb_ref[...] = v_ref[...].astype(BF16)
        ckb_ref[...] = ck_ref[...].astype(BF16)
        cvb_ref[...] = cv_ref[...].astype(BF16)

    w0 = pl.multiple_of(_na_window_start(blk) * GRID_W, GRID_W)
    nk = NA_KROWS * GRID_W
    for hh, sl in enumerate(heads):
        qn = (_head_rms(q_ref[:, sl], qw_ref[...]) * scale).astype(BF16)
        s_loc = _nt_dot(qn, kn_ref[pl.ds(w0, nk), sl]) + bias_ref[hh]
        s_ctx = _nt_dot(qn, ckb_ref[:, sl])
        m = jnp.maximum(jnp.max(s_loc, axis=-1, keepdims=True), jnp.max(s_ctx, axis=-1, keepdims=True))
        e_loc = jnp.exp(s_loc - m)
        e_ctx = jnp.exp(s_ctx - m)
        den = jnp.sum(e_loc, axis=-1, keepdims=True) + jnp.sum(e_ctx, axis=-1, keepdims=True)
        o = (jnp.dot(e_loc.astype(BF16), vb_ref[pl.ds(w0, nk), sl], preferred_element_type=F32)
             + jnp.dot(e_ctx.astype(BF16), cvb_ref[:, sl], preferred_element_type=F32))
        o_ref[:, sl] = (o / den).astype(o_ref.dtype)


def _na_bias_table(rpb):
    nh, nblk, w = rpb.shape[0], GRID_ROWS // NA_QROWS, GRID_W
    v = jnp.pad(rpb, ((0, 0), (0, 0), (w - NA_COLS, w - NA_COLS + 1)))
    toe = jnp.tile(v, (1, 1, w))[:, :, :w * (2 * w - 1)].reshape(nh, -1, w, 2 * w - 1)[..., w - 1:]
    qc, kc = np.arange(w)[:, None], np.arange(w)[None, :]
    cs = np.clip(qc - NA_COLS // 2, 0, w - NA_COLS)
    toe = jnp.where(((kc >= cs) & (kc < cs + NA_COLS))[None, None], toe, NEG)
    masked = jnp.full((nh, w, w), NEG, F32)
    tabs = []
    for blk in (0, 1, nblk - 1):
        w0 = int(np.clip(NA_QROWS * blk - NA_ROWS // 2, 0, GRID_ROWS - NA_KROWS))
        qrows = []
        for a in range(NA_QROWS):
            qr = NA_QROWS * blk + a
            rs = int(np.clip(qr - NA_ROWS // 2, 0, GRID_ROWS - NA_ROWS))
            krows = [toe[:, w0 + t - qr + NA_ROWS - 1] if rs <= w0 + t < rs + NA_ROWS else masked
                     for t in range(NA_KROWS)]
            qrows.append(jnp.stack(krows, axis=2))
        tabs.append(jnp.stack(qrows, axis=1).reshape(nh, NA_QROWS * w, NA_KROWS * w))
    return jnp.stack(tabs, axis=1)


def neighbourhood_attention(proj, cache_k4, cache_v4, j, rpb, q_norm_w, k_norm_w, mix):
    nblk = GRID_ROWS // NA_QROWS
    tq = NA_QROWS * GRID_W
    tk = NA_KROWS * GRID_W
    bias = _na_bias_table(rpb)
    qrow0 = N_PROMPT // tq
    krow0 = N_PROMPT // DEC_SEQ

    def variant(blk):
        return jnp.where(blk == 0, 0, jnp.where(blk == nblk - 1, 2, 1))

    hd = NA_HEAD_DIM
    nhs = NA_HEADS_PER_STEP
    bw = nhs * hd
    ng = NA_HEADS // nhs
    return pl.pallas_call(
        _na_attn_body,
        grid=(DEC_BATCH, ng, nblk),
        in_specs=[
            pl.BlockSpec((tq, bw), lambda b, h, t: (qrow0 + b * nblk + t, h)),
            pl.BlockSpec((DEC_SEQ, bw), lambda b, h, t: (krow0 + b, ng + h)),
            pl.BlockSpec((DEC_SEQ, bw), lambda b, h, t: (krow0 + b, 2 * ng + h)),
            pl.BlockSpec((None, None, PAST_LEN, bw), lambda b, h, t: (b, j, 0, h)),
            pl.BlockSpec((None, None, PAST_LEN, bw), lambda b, h, t: (b, j, 0, h)),
            pl.BlockSpec((nhs, None, tq, tk), lambda b, h, t: (h, variant(t), 0, 0)),
            pl.BlockSpec((1, hd), lambda b, h, t: (0, 0)),
            pl.BlockSpec((1, hd), lambda b, h, t: (0, 0)),
            pl.BlockSpec(memory_space=pl.ANY),
        ],
        out_specs=pl.BlockSpec((tq, bw), lambda b, h, t: (qrow0 + b * nblk + t, h)),
        out_shape=jax.ShapeDtypeStruct(mix.shape, mix.dtype),
        input_output_aliases={8: 0},
        scratch_shapes=[pltpu.VMEM((DEC_SEQ, bw), BF16), pltpu.VMEM((DEC_SEQ, bw), BF16),
                        pltpu.VMEM((PAST_LEN, bw), BF16), pltpu.VMEM((PAST_LEN, bw), BF16)],
        compiler_params=_cparams(("parallel", "parallel", "arbitrary")),
        name="neighbourhood_attention",
    )(proj, proj, proj, cache_k4, cache_v4, bias, q_norm_w.reshape(1, -1), k_norm_w.reshape(1, -1), mix)


def _tile_seq_pos(c, rt):
    n_p = N_PROMPT // rt
    pp, ps = SEQ // rt, DEC_SEQ // rt
    pos = jnp.where(c < n_p, c % pp, (c - n_p) % ps)
    per = jnp.where(c < n_p, pp, ps)
    return pos == 0, pos == per - 1


def _halo_specs(rt, width, col_of, halo):
    per = rt // halo
    last = N_TOK // halo - 1
    return [
        pl.BlockSpec((halo, width), lambda c, j: (jnp.maximum(c * per - 1, 0), col_of(j))),
        pl.BlockSpec((rt, width), lambda c, j: (c, col_of(j))),
        pl.BlockSpec((halo, width), lambda c, j: (jnp.minimum((c + 1) * per, last), col_of(j))),
    ]


def _ssd_conv_body(prev_ref, cur_ref, next_ref, w_ref, b_ref, o_ref, ext_ref, *, rt):
    first, last = _tile_seq_pos(pl.program_id(0), rt)
    ext_ref[0:HALO] = jnp.where(first, 0.0, prev_ref[...])
    ext_ref[HALO:HALO + rt] = cur_ref[...]
    ext_ref[HALO + rt:] = jnp.where(last, 0.0, next_ref[...])
    pad = SSD_CONV_W // 2
    ext = ext_ref[...]
    acc = b_ref[...]
    for k in range(SSD_CONV_W):
        shifted = ext if k == pad else pltpu.roll(ext, (pad - k) % ext.shape[0], 0)
        acc = acc + shifted[HALO:HALO + rt] * w_ref[k:k + 1]
    o_ref[...] = _silu(acc)


def ssd_conv(proj, conv_w, conv_b, rt=ROW_TILE, width=1024):
    col0 = (3 * NA_WIDTH + SSD_INNER) // width
    return pl.pallas_call(
        functools.partial(_ssd_conv_body, rt=rt),
        grid=(N_TOK // rt, SSD_CONV_DIM // width),
        in_specs=[
            *_halo_specs(rt, width, lambda j: col0 + j, HALO),
            pl.BlockSpec((SSD_CONV_W, width), lambda c, j: (0, j)),
            pl.BlockSpec((1, width), lambda c, j: (0, j)),
        ],
        out_specs=pl.BlockSpec((rt, width), lambda c, j: (c, j)),
        out_shape=jax.ShapeDtypeStruct((N_TOK, SSD_CONV_DIM), F32),
        scratch_shapes=[pltpu.VMEM((rt + 2 * HALO, width), F32)],
        compiler_params=_cparams(("parallel", "parallel")),
        name="ssd_conv",
    )(proj, proj, proj, conv_w, conv_b.reshape(1, -1))


def _short_conv_body(gb_ref, cp_ref, cc_ref, cn_ref, xp_ref, xc_ref, xn_ref, w_ref, o_ref, ext_ref, *, rt):
    first, last = _tile_seq_pos(pl.program_id(0), rt)

    def prod(c_ref, x_ref):
        return c_ref[...].astype(F32) * x_ref[...].astype(F32)

    h = HALO_BF16
    ext_ref[0:h] = jnp.where(first, 0.0, prod(cp_ref, xp_ref))
    ext_ref[h:h + rt] = prod(cc_ref, xc_ref)
    ext_ref[h + rt:] = jnp.where(last, 0.0, prod(cn_ref, xn_ref))
    pad = SC_CONV_W // 2
    ext = ext_ref[...]
    acc = None
    for k in range(SC_CONV_W):
        shifted = ext if k == pad else pltpu.roll(ext, (pad - k) % ext.shape[0], 0)
        term = shifted[h:h + rt] * w_ref[k:k + 1]
        acc = term if acc is None else acc + term
    o_ref[...] = (gb_ref[...].astype(F32) * acc).astype(o_ref.dtype)


def short_conv_gate(proj, conv_w, rt=ROW_TILE, width=1024):
    nb = SC_WIDTH // width
    return pl.pallas_call(
        functools.partial(_short_conv_body, rt=rt),
        grid=(N_TOK // rt, nb),
        in_specs=[
            pl.BlockSpec((rt, width), lambda c, j: (c, j)),
            *_halo_specs(rt, width, lambda j: nb + j, HALO_BF16),
            *_halo_specs(rt, width, lambda j: 2 * nb + j, HALO_BF16),
            pl.BlockSpec((SC_CONV_W, width), lambda c, j: (0, j)),
        ],
        out_specs=pl.BlockSpec((rt, width), lambda c, j: (c, j)),
        out_shape=jax.ShapeDtypeStruct((N_TOK, SC_WIDTH), BF16),
        scratch_shapes=[pltpu.VMEM((rt + 2 * HALO_BF16, width), F32)],
        compiler_params=_cparams(("parallel", "parallel")),
        name="short_conv_gate",
    )(proj, proj, proj, proj, proj, proj, proj, conv_w)


def _chunk_cumsum(x, reverse):
    rows = lax.broadcasted_iota(jnp.int32, x.shape, 0)
    n = x.shape[0]
    k = 1
    while k < n:
        if reverse:
            x = x + jnp.where(rows < n - k, pltpu.roll(x, n - k, 0), 0.0)
        else:
            x = x + jnp.where(rows >= k, pltpu.roll(x, k, 0), 0.0)
        k *= 2
    return x


def _dt_and_logdecay(dt_ref, bias_ref, a_ref):
    dtv = _softplus(dt_ref[...] + bias_ref[...])
    dta = dtv * a_ref[...]
    lane = lax.broadcasted_iota(jnp.int32, dta.shape, 1)
    acum = jnp.where(lane < SSD_HEADS, _chunk_cumsum(dta, False), _chunk_cumsum(dta, True))
    return dtv, acum


def _pair_cols(m, c0, lane):
    shape = (m.shape[0], LANES)
    return jnp.where(lane[:m.shape[0]] < SSD_HEAD_DIM,
                     jnp.broadcast_to(m[:, c0:c0 + 1], shape),
                     jnp.broadcast_to(m[:, c0 + 1:c0 + 2], shape))


def _ssd_state_body(xf_ref, bf_ref, dtf_ref, xb_ref, bb_ref, dtb_ref, h0_ref, bias_ref, a_ref, *rest, nc, final):
    h_ref = rest[-1]
    hpf_ref, hpb_ref = rest[-4:-2] if final else rest[-3:-1]
    t = pl.program_id(1)

    @pl.when(t == 0)
    def _():
        h_ref[...] = h0_ref[...]

    lane = lax.broadcasted_iota(jnp.int32, (SSD_CHUNK, LANES), 1)
    for d, (x_ref, b_ref, dt_ref, hp_ref) in enumerate(
            ((xf_ref, bf_ref, dtf_ref, hpf_ref), (xb_ref, bb_ref, dtb_ref, hpb_ref))):
        dtv, acum = _dt_and_logdecay(dt_ref, bias_ref, a_ref)
        edge = acum[SSD_CHUNK - 1:SSD_CHUNK] if d == 0 else acum[0:1]
        s = dtv * jnp.exp(edge - acum)
        cd = jnp.exp(edge)
        hp_ref[...] = h_ref[d].astype(BF16)
        for g in range(SSD_GROUPS):
            bt = jnp.transpose(b_ref[:, g * SSD_STATE:(g + 1) * SSD_STATE]).astype(BF16)
            for pr in range(SSD_HPG // 2):
                head = g * SSD_HPG + 2 * pr
                c0 = d * SSD_HEADS + head
                cols = slice(head * SSD_HEAD_DIM, head * SSD_HEAD_DIM + LANES)
                xd = (x_ref[:, cols] * _pair_cols(s, c0, lane)).astype(BF16)
                st = jnp.dot(bt, xd, preferred_element_type=F32)
                h_ref[d, :, cols] = h_ref[d, :, cols] * _pair_cols(cd, c0, lane) + st

    if final:
        hfin_ref = rest[-2]

        @pl.when(t == nc - 1)
        def _():
            for d in range(2):
                for c in range(0, SSD_INNER, LANES):
                    hfin_ref[d, c:c + LANES, :] = jnp.transpose(h_ref[d, :, c:c + LANES])


def ssd_states(xbc, dt_raw, h0t, dt_bias, a_neg, row0, nseq, nc, final=None):
    c0 = row0 // SSD_CHUNK
    hp = SSD_INNER
    bcol = SSD_INNER // (SSD_GROUPS * SSD_STATE)
    out_specs = [
        pl.BlockSpec((None, None, SSD_STATE, hp), lambda s, t: (s, t, 0, 0)),
        pl.BlockSpec((None, None, SSD_STATE, hp), lambda s, t: (s, nc - 1 - t, 0, 0)),
    ]
    out_shape = [jax.ShapeDtypeStruct((nseq, nc, SSD_STATE, hp), BF16)] * 2
    carried, aliases = (), {}
    if final is not None:
        j, n_even, earlier = final
        out_specs.append(pl.BlockSpec((None, None, 2, hp, SSD_STATE), lambda s, t: (s, j, 0, 0, 0)))
        out_shape.append(jax.ShapeDtypeStruct((nseq, n_even, 2, hp, SSD_STATE), F32))
        if earlier is not None:
            carried, aliases = (earlier,), {9: 2}

    def fwd(s, t):
        return c0 + s * nc + t

    def bwd(s, t):
        return c0 + s * nc + (nc - 1 - t)

    return pl.pallas_call(
        functools.partial(_ssd_state_body, nc=nc, final=final is not None),
        grid=(nseq, nc),
        in_specs=[
            pl.BlockSpec((SSD_CHUNK, hp), lambda s, t: (fwd(s, t), 0)),
            pl.BlockSpec((SSD_CHUNK, SSD_GROUPS * SSD_STATE), lambda s, t: (fwd(s, t), bcol)),
            pl.BlockSpec((SSD_CHUNK, LANES), lambda s, t: (fwd(s, t), 0)),
            pl.BlockSpec((SSD_CHUNK, hp), lambda s, t: (bwd(s, t), 0)),
            pl.BlockSpec((SSD_CHUNK, SSD_GROUPS * SSD_STATE), lambda s, t: (bwd(s, t), bcol)),
            pl.BlockSpec((SSD_CHUNK, LANES), lambda s, t: (bwd(s, t), 0)),
            pl.BlockSpec((None, 2, SSD_STATE, hp), lambda s, t: (s, 0, 0, 0)),
            pl.BlockSpec((1, LANES), lambda s, t: (0, 0)),
            pl.BlockSpec((1, LANES), lambda s, t: (0, 0)),
            *[pl.BlockSpec(memory_space=pl.ANY) for _ in carried],
        ],
        out_specs=out_specs,
        out_shape=out_shape,
        input_output_aliases=aliases,
        scratch_shapes=[pltpu.VMEM((2, SSD_STATE, hp), F32)],
        compiler_params=_cparams(("parallel", "arbitrary")),
        name="ssd_states",
    )(xbc, xbc, dt_raw, xbc, xbc, dt_raw, h0t, dt_bias, a_neg, *carried)


def _ssd_out_body(x_ref, b_ref, c_ref, z_ref, dt_ref, hpf_ref, hpb_ref, bias_ref, a_ref, dsk_ref, nw_ref,
                  mix_ref, o_ref, g_ref):
    del mix_ref
    dtv, acum = _dt_and_logdecay(dt_ref, bias_ref, a_ref)
    acum_t = jnp.transpose(acum)
    dt_t = jnp.transpose(dtv)
    shape = (SSD_CHUNK, SSD_CHUNK)
    rows = lax.broadcasted_iota(jnp.int32, shape, 0)
    lane = lax.broadcasted_iota(jnp.int32, shape, 1)
    lower = rows >= lane
    upper = rows <= lane

    def head_terms(cb, head):
        cf, cbk = head, SSD_HEADS + head
        col_f = jnp.broadcast_to(acum[:, cf:cf + 1], shape)
        col_b = jnp.broadcast_to(acum[:, cbk:cbk + 1], shape)
        lf = jnp.exp(jnp.where(lower, col_f - acum_t[cf:cf + 1, :], -jnp.inf)) * dt_t[cf:cf + 1, :]
        lb = jnp.exp(jnp.where(upper, col_b - acum_t[cbk:cbk + 1, :], -jnp.inf)) * dt_t[cbk:cbk + 1, :]
        return (cb * (lf + lb)).astype(BF16), jnp.exp(col_f), jnp.exp(col_b)

    for g in range(SSD_GROUPS):
        gcols = slice(g * SSD_GROUP_W, (g + 1) * SSD_GROUP_W)
        bg = b_ref[:, g * SSD_STATE:(g + 1) * SSD_STATE].astype(BF16)
        cg = c_ref[:, g * SSD_STATE:(g + 1) * SSD_STATE].astype(BF16)
        cb = _nt_dot(cg, bg)
        yoff_f = jnp.dot(cg, hpf_ref[:, gcols], preferred_element_type=F32)
        yoff_b = jnp.dot(cg, hpb_ref[:, gcols], preferred_element_type=F32)
        for pr in range(SSD_HPG // 2):
            head = g * SSD_HPG + 2 * pr
            cols = slice(head * SSD_HEAD_DIM, head * SSD_HEAD_DIM + LANES)
            pcols = slice(pr * LANES, (pr + 1) * LANES)
            xp = x_ref[:, cols]
            xb = xp.astype(BF16)
            first = lane < SSD_HEAD_DIM
            w0, ef0, eb0 = head_terms(cb, head)
            w1, ef1, eb1 = head_terms(cb, head + 1)
            y = jnp.where(first, jnp.dot(w0, xb, preferred_element_type=F32),
                          jnp.dot(w1, xb, preferred_element_type=F32))
            y = y + jnp.where(first, ef0, ef1) * yoff_f[:, pcols]
            y = y + jnp.where(first, eb0, eb1) * yoff_b[:, pcols]
            y = y + xp * dsk_ref[:, cols]
            g_ref[:, cols] = y * _silu(z_ref[:, cols])
        gg = g_ref[:, gcols]
        ms = jnp.mean(gg * gg, axis=-1, keepdims=True)
        o_ref[:, gcols] = (gg * lax.rsqrt(ms + EPS) * nw_ref[:, gcols]).astype(o_ref.dtype)


def ssd_outputs(xbc, proj, dt_raw, hpf, hpb, dt_bias, a_neg, d_skip, norm_w, row0, nseq, nc, mix):
    c0 = row0 // SSD_CHUNK
    hp = SSD_INNER
    gw = SSD_GROUPS * SSD_STATE
    bcol = SSD_INNER // gw
    zcol = 3 * NA_WIDTH // SSD_INNER

    def row(s, t):
        return c0 + s * nc + t

    return pl.pallas_call(
        _ssd_out_body,
        grid=(nseq, nc),
        in_specs=[
            pl.BlockSpec((SSD_CHUNK, hp), lambda s, t: (row(s, t), 0)),
            pl.BlockSpec((SSD_CHUNK, gw), lambda s, t: (row(s, t), bcol)),
            pl.BlockSpec((SSD_CHUNK, gw), lambda s, t: (row(s, t), bcol + 1)),
            pl.BlockSpec((SSD_CHUNK, hp), lambda s, t: (row(s, t), zcol)),
            pl.BlockSpec((SSD_CHUNK, LANES), lambda s, t: (row(s, t), 0)),
            pl.BlockSpec((None, None, SSD_STATE, hp), lambda s, t: (s, t, 0, 0)),
            pl.BlockSpec((None, None, SSD_STATE, hp), lambda s, t: (s, t, 0, 0)),
            pl.BlockSpec((1, LANES), lambda s, t: (0, 0)),
            pl.BlockSpec((1, LANES), lambda s, t: (0, 0)),
            pl.BlockSpec((1, hp), lambda s, t: (0, 0)),
            pl.BlockSpec((1, hp), lambda s, t: (0, 0)),
            pl.BlockSpec(memory_space=pl.ANY),
        ],
        out_specs=pl.BlockSpec((SSD_CHUNK, hp), lambda s, t: (row(s, t), NA_WIDTH // hp)),
        out_shape=jax.ShapeDtypeStruct(mix.shape, mix.dtype),
        input_output_aliases={11: 0},
        scratch_shapes=[pltpu.VMEM((SSD_CHUNK, hp), F32)],
        compiler_params=_cparams(("parallel", "parallel")),
        name="ssd_outputs",
    )(xbc, xbc, xbc, proj, dt_raw, hpf, hpb, dt_bias, a_neg, d_skip, norm_w, mix)


def _pad_lanes(v):
    return jnp.pad(v.reshape(1, -1), ((0, 0), (0, LANES - v.size)))


def ssd_mixer(proj, dt_raw, state_h0, conv_w, conv_b, a_log, dt_bias, d_skip, norm_w, mix, final):
    xbc = ssd_conv(proj, conv_w, conv_b)
    bias = _pad_lanes(dt_bias)
    a_neg = _pad_lanes(-jnp.exp(a_log))
    dsk = jnp.repeat(d_skip, SSD_HEAD_DIM).reshape(1, -1)
    nw = norm_w.reshape(1, -1)

    def to_t(h):
        return jnp.transpose(h, (0, 1, 4, 2, 3)).reshape(h.shape[0], 2, SSD_STATE, SSD_INNER)

    nc_p, nc_s = SEQ // SSD_CHUNK, DEC_SEQ // SSD_CHUNK
    h0_p = jnp.zeros((BATCH, 2, SSD_STATE, SSD_INNER), F32)
    hpf, hpb, hfin = ssd_states(xbc, dt_raw, h0_p, bias, a_neg, 0, BATCH, nc_p, final)
    mix = ssd_outputs(xbc, proj, dt_raw, hpf, hpb, bias, a_neg, dsk, nw, 0, BATCH, nc_p, mix)
    hpf, hpb = ssd_states(xbc, dt_raw, to_t(state_h0), bias, a_neg, N_PROMPT, DEC_BATCH, nc_s)
    mix = ssd_outputs(xbc, proj, dt_raw, hpf, hpb, bias, a_neg, dsk, nw, N_PROMPT, DEC_BATCH, nc_s, mix)
    return mix, hfin


def kernel(x_prompt, x_sample, cache_k, cache_v, state_ssm, c, c_ctx, norm1_w, norm2_w, w_mod, b_mod,
           w_in_even, w_out_even, q_norm_w, k_norm_w, na_rel_bias, ssd_conv_w, ssd_conv_b, ssd_a_log,
           ssd_dt_bias, ssd_d, ssd_norm_w, w_in_odd, sc_conv_w, w_out_odd, w_ffn_gate, w_ffn_up,
           w_ffn_down):
    d = D_MODEL
    x = (x_prompt.reshape(N_PROMPT, d), x_sample.reshape(N_SAMPLE, d))
    cond8 = jnp.concatenate([c_ctx[None, :], c, jnp.zeros((8 - 1 - DEC_BATCH, d), F32)], axis=0)
    mods = adaln_table(cond8, w_mod, b_mod)
    n_even = w_in_even.shape[0]
    cache_k4 = cache_k.reshape(DEC_BATCH, n_even, PAST_LEN, NA_WIDTH)
    cache_v4 = cache_v.reshape(DEC_BATCH, n_even, PAST_LEN, NA_WIDTH)

    w_down = w_ffn_down.astype(BF16)
    caches, states = None, None
    for l in range(DEPTH):
        j = l // 2
        h = modnorm(x, norm1_w[l], mods, l, 0)
        if l % 2 == 0:
            w_dt = jnp.pad(w_in_even[j:j + 1, :, EVEN_MAIN:], ((0, 0), (0, 0), (0, LANES - 2 * SSD_HEADS)))
            pr = proj(h, w_in_even, j, EVEN_MAIN, F32)
            dt_raw = proj(h, w_dt, 0, LANES, F32, tn=LANES)
            mix, *caches = dense_attention(pr, q_norm_w[j], k_norm_w[j], j, n_even, caches)
            mix = neighbourhood_attention(pr, cache_k4, cache_v4, j, na_rel_bias[j],
                                          q_norm_w[j], k_norm_w[j], mix)
            mix, states = ssd_mixer(pr, dt_raw, state_ssm[:, j], ssd_conv_w[j], ssd_conv_b[j],
                                    ssd_a_log[j].reshape(-1), ssd_dt_bias[j].reshape(-1), ssd_d[j],
                                    ssd_norm_w[j], mix, (j, n_even, states))
            x_mid = mixer_out_proj(mix, w_out_even, j, x, mods, l)
        else:
            pr = proj(h, w_in_odd, j, 3 * SC_WIDTH, BF16)
            mix = short_conv_gate(pr, sc_conv_w[j])
            x_mid = mixer_out_proj(mix, w_out_odd, j, x, mods, l)
        h = modnorm((x_mid,), norm2_w[l], mods, l, 3)
        hidden = swiglu(h, w_ffn_gate, w_ffn_up, l)
        if l < DEPTH - 1:
            x = (ffn_down_proj(hidden, w_down, x_mid, mods, l),)
        else:
            y_prompt = ffn_down_proj(hidden, w_down, x_mid, mods, l, 0, N_PROMPT).reshape(BATCH, SEQ, d)
            y_sample = ffn_down_proj(hidden, w_down, x_mid, mods, l, N_PROMPT, N_SAMPLE).reshape(DEC_BATCH, DEC_SEQ, d)

    new_k, new_v = (t.reshape(BATCH, n_even, SEQ, NA_HEADS, NA_HEAD_DIM) for t in caches)
    new_h = states.reshape(BATCH, n_even, 2, SSD_HEADS, SSD_HEAD_DIM, SSD_STATE)
    return (y_prompt, y_sample, new_k, new_v, new_h)
```

```python
import functools
import math

import numpy as np
import jax
import jax.numpy as jnp
from jax import lax
from jax.experimental import pallas as pl
from jax.experimental.pallas import tpu as pltpu

D_MODEL = 4096
BATCH, SEQ = 32, 256
DEC_BATCH, DEC_SEQ = 4, 4096
PAST_LEN = 256
DEPTH = 4
GRID_W = 64
GRID_ROWS = DEC_SEQ // GRID_W
NA_HEADS, NA_HEAD_DIM = 16, 128
NA_WIDTH = NA_HEADS * NA_HEAD_DIM
NA_ROWS, NA_COLS = 8, 16
SSD_HEAD_DIM = 64
SSD_INNER = D_MODEL // 2
SSD_HEADS = SSD_INNER // SSD_HEAD_DIM
SSD_GROUPS = 4
SSD_HPG = SSD_HEADS // SSD_GROUPS
SSD_STATE = 128
SSD_CONV_W = 5
SSD_CHUNK = 128
SSD_CONV_DIM = SSD_INNER + 2 * SSD_GROUPS * SSD_STATE
SSD_GROUP_W = SSD_HPG * SSD_HEAD_DIM
EVEN_MAIN = 3 * NA_WIDTH + SSD_INNER + SSD_CONV_DIM
SC_WIDTH = D_MODEL
SC_CONV_W = 3
EPS = 1e-6

N_PROMPT = BATCH * SEQ
N_SAMPLE = DEC_BATCH * DEC_SEQ
N_TOK = N_PROMPT + N_SAMPLE

LANES = 128
SUBLANES = 8
VMEM_LIMIT = 56 * 1024 * 1024
ROW_TILE = 256
HALO = SUBLANES
HALO_BF16 = 2 * SUBLANES
NA_QROWS = 4
NA_KROWS = NA_QROWS + NA_ROWS
NA_HEADS_PER_STEP = 2
NEG = -1e30

F32 = jnp.float32
BF16 = jnp.bfloat16


def _cparams(sem):
    return pltpu.CompilerParams(dimension_semantics=sem, vmem_limit_bytes=VMEM_LIMIT)


def _silu(x):
    return x / (1.0 + jnp.exp(-x))


def _softplus(x):
    return jnp.maximum(x, 0.0) + jnp.log(1.0 + jnp.exp(-jnp.abs(x)))


def _cond_row(i, tm):
    n_p = N_PROMPT // tm
    per = DEC_SEQ // tm
    return jnp.where(i < n_p, 0, 1 + (i - n_p) // per)


def _adaln_body(c_ref, w_ref, b_ref, o_ref):
    a = _silu(c_ref[...]).astype(BF16)
    o_ref[...] = jnp.dot(a, w_ref[...].astype(BF16), preferred_element_type=F32) + b_ref[...]


def adaln_table(cond8, w_mod, b_mod):
    depth, d, n6 = w_mod.shape
    tn = 512
    out = pl.pallas_call(
        _adaln_body,
        grid=(depth, n6 // tn),
        in_specs=[
            pl.BlockSpec((8, d), lambda l, j: (0, 0)),
            pl.BlockSpec((None, d, tn), lambda l, j: (l, 0, j)),
            pl.BlockSpec((None, 1, tn), lambda l, j: (l, 0, j)),
        ],
        out_specs=pl.BlockSpec((None, 8, tn), lambda l, j: (l, 0, j)),
        out_shape=jax.ShapeDtypeStruct((depth, 8, n6), F32),
        compiler_params=_cparams(("parallel", "parallel")),
        name="adaln_table",
    )(cond8, w_mod, b_mod.reshape(depth, 1, n6))
    return out.reshape(depth, 8, 6, 1, d)


def _row_part_specs(parts, tm, tn, ids):
    if len(parts) == 1:
        return [pl.BlockSpec((tm, tn), lambda *g: ids(*g))]
    n_p = N_PROMPT // tm
    return [
        pl.BlockSpec((tm, tn), lambda *g: (jnp.minimum(ids(*g)[0], n_p - 1), ids(*g)[1])),
        pl.BlockSpec((tm, tn), lambda *g: (jnp.maximum(ids(*g)[0] - n_p, 0), ids(*g)[1])),
    ]


def _modnorm_body(*refs, tm):
    *x_refs, g_ref, sh_ref, sc_ref, h_ref, gain_ref = refs
    gain_ref[...] = g_ref[...] * (1.0 + sc_ref[...])
    d = h_ref.shape[1]
    rows = 8 * SUBLANES
    chunk = 2 * LANES

    def run(x_ref):
        def body(r, carry):
            sl = pl.ds(pl.multiple_of(r * rows, rows), rows)
            acc = jnp.zeros((rows, LANES), F32)
            for c in range(0, d, LANES):
                xc = x_ref[sl, c:c + LANES]
                acc = acc + xc * xc
            rinv = lax.rsqrt(jnp.sum(acc, axis=-1, keepdims=True) / d + EPS)
            for c in range(0, d, chunk):
                cs = slice(c, c + chunk)
                h_ref[sl, cs] = (x_ref[sl, cs] * rinv * gain_ref[:, cs] + sh_ref[:, cs]).astype(h_ref.dtype)
            return carry

        lax.fori_loop(0, tm // rows, body, 0)

    if len(x_refs) == 1:
        run(x_refs[0])
    else:
        in_prompt = pl.program_id(0) < N_PROMPT // tm
        pl.when(in_prompt)(lambda: run(x_refs[0]))
        pl.when(jnp.logical_not(in_prompt))(lambda: run(x_refs[1]))


def modnorm(x_parts, norm_w, mods, layer, which_shift, tm=2 * ROW_TILE):
    d = D_MODEL

    def mod_spec(which):
        return pl.BlockSpec((None, None, None, 1, d), lambda i: (layer, _cond_row(i, tm), which, 0, 0))

    return pl.pallas_call(
        functools.partial(_modnorm_body, tm=tm),
        grid=(N_TOK // tm,),
        in_specs=[
            *_row_part_specs(x_parts, tm, d, lambda i: (i, 0)),
            pl.BlockSpec((1, d), lambda i: (0, 0)),
            mod_spec(which_shift),
            mod_spec(which_shift + 1),
        ],
        out_specs=pl.BlockSpec((tm, d), lambda i: (i, 0)),
        out_shape=jax.ShapeDtypeStruct((N_TOK, d), BF16),
        scratch_shapes=[pltpu.VMEM((1, d), F32)],
        compiler_params=_cparams(("parallel",)),
        name="modnorm",
    )(*x_parts, norm_w.reshape(1, d), mods, mods)


def _cast_weight_once(w_ref, wb_ref):
    @pl.when(pl.program_id(1) == 0)
    def _():
        wb_ref[...] = w_ref[...].astype(BF16)


def _proj_body(h_ref, w_ref, o_ref, wb_ref):
    _cast_weight_once(w_ref, wb_ref)
    o_ref[...] = jnp.dot(h_ref[...], wb_ref[...], preferred_element_type=F32).astype(o_ref.dtype)


def _weight_spec(kdim, tn, layer):
    return pl.BlockSpec((None, kdim, tn), lambda j, i: (layer, 0, j), pipeline_mode=pl.Buffered(1))


def proj(h, w3, widx, n, out_dtype, tn=1024, tm=1024):
    m, d = h.shape
    return pl.pallas_call(
        _proj_body,
        grid=(pl.cdiv(n, tn), m // tm),
        in_specs=[
            pl.BlockSpec((tm, d), lambda j, i: (i, 0)),
            _weight_spec(d, tn, widx),
        ],
        out_specs=pl.BlockSpec((tm, tn), lambda j, i: (i, j)),
        out_shape=jax.ShapeDtypeStruct((m, n), out_dtype),
        scratch_shapes=[pltpu.VMEM((d, tn), BF16)],
        compiler_params=_cparams(("parallel", "arbitrary")),
        name="proj",
    )(h, w3)


def _swiglu_body(h_ref, wg_ref, wu_ref, o_ref, wgb_ref, wub_ref):
    _cast_weight_once(wg_ref, wgb_ref)
    _cast_weight_once(wu_ref, wub_ref)
    h = h_ref[...]
    a = jnp.dot(h, wgb_ref[...], preferred_element_type=F32)
    b = jnp.dot(h, wub_ref[...], preferred_element_type=F32)
    o_ref[...] = (_silu(a) * b).astype(o_ref.dtype)


def swiglu(h, wg3, wu3, layer, tn=512, tm=1024):
    m, d = h.shape
    n = wg3.shape[2]
    wspec = _weight_spec(d, tn, layer)
    return pl.pallas_call(
        _swiglu_body,
        grid=(pl.cdiv(n, tn), m // tm),
        in_specs=[pl.BlockSpec((tm, d), lambda j, i: (i, 0)), wspec, wspec],
        out_specs=pl.BlockSpec((tm, tn), lambda j, i: (i, j)),
        out_shape=jax.ShapeDtypeStruct((m, n), BF16),
        scratch_shapes=[pltpu.VMEM((d, tn), BF16), pltpu.VMEM((d, tn), BF16)],
        compiler_params=_cparams(("parallel", "arbitrary")),
        name="swiglu",
    )(h, wg3, wu3)


def _mixer_out_body(a_ref, w_ref, g_ref, *refs, tm):
    *x_refs, o_ref, wb_ref = refs
    _cast_weight_once(w_ref, wb_ref)
    if len(x_refs) == 1:
        x = x_refs[0][...]
    else:
        x = jnp.where(pl.program_id(1) < N_PROMPT // tm, x_refs[0][...], x_refs[1][...])
    o_ref[...] = x + g_ref[...] * jnp.dot(a_ref[...], wb_ref[...], preferred_element_type=F32)


def mixer_out_proj(a, w3, widx, x_parts, mods, layer, tn=1024, tm=512):
    m, kdim = a.shape
    n = w3.shape[2]
    return pl.pallas_call(
        functools.partial(_mixer_out_body, tm=tm),
        grid=(n // tn, m // tm),
        in_specs=[
            pl.BlockSpec((tm, kdim), lambda j, i: (i, 0)),
            _weight_spec(kdim, tn, widx),
            pl.BlockSpec((None, None, None, 1, tn), lambda j, i: (layer, _cond_row(i, tm), 2, 0, j)),
            *_row_part_specs(x_parts, tm, tn, lambda j, i: (i, j)),
        ],
        out_specs=pl.BlockSpec((tm, tn), lambda j, i: (i, j)),
        out_shape=jax.ShapeDtypeStruct((m, n), F32),
        scratch_shapes=[pltpu.VMEM((kdim, tn), BF16)],
        compiler_params=_cparams(("parallel", "arbitrary")),
        name="mixer_out_proj",
    )(a, w3, mods, *x_parts)


def _cast_body(w_ref, o_ref):
    o_ref[...] = w_ref[...].astype(o_ref.dtype)


def cast_weights_bf16(w3, tk=512):
    layers, kdim, n = w3.shape
    spec = pl.BlockSpec((None, tk, n), lambda l, k: (l, k, 0))
    return pl.pallas_call(
        _cast_body,
        grid=(layers, pl.cdiv(kdim, tk)),
        in_specs=[spec],
        out_specs=spec,
        out_shape=jax.ShapeDtypeStruct(w3.shape, BF16),
        compiler_params=_cparams(("parallel", "parallel")),
        name="cast_weights_bf16",
    )(w3)


def _ffn_down_body(a_ref, w_ref, x_ref, g_ref, o_ref):
    o_ref[...] = x_ref[...] + g_ref[...] * jnp.dot(a_ref[...], w_ref[...], preferred_element_type=F32)


def ffn_down_proj(a, w3, x, mods, layer, row0=0, nrows=None, tn=512, tm=512):
    kdim = a.shape[1]
    n = w3.shape[2]
    nrows = a.shape[0] if nrows is None else nrows
    i0 = row0 // tm
    return pl.pallas_call(
        _ffn_down_body,
        grid=(nrows // tm, n // tn),
        in_specs=[
            pl.BlockSpec((tm, kdim), lambda i, j: (i0 + i, 0)),
            pl.BlockSpec((None, kdim, tn), lambda i, j: (layer, 0, j)),
            pl.BlockSpec((tm, tn), lambda i, j: (i0 + i, j)),
            pl.BlockSpec((None, None, None, 1, tn), lambda i, j: (layer, _cond_row(i0 + i, tm), 5, 0, j)),
        ],
        out_specs=pl.BlockSpec((tm, tn), lambda i, j: (i, j)),
        out_shape=jax.ShapeDtypeStruct((nrows, n), F32),
        compiler_params=_cparams(("parallel", "parallel")),
        name="ffn_down_proj",
    )(a, w3, x, mods)


def _head_rms(x, w):
    ms = jnp.mean(x * x, axis=-1, keepdims=True)
    return x * lax.rsqrt(ms + EPS) * w


def _nt_dot(a, b):
    return lax.dot_general(a, b, (((1,), (1,)), ((), ())), preferred_element_type=F32)


def _dense_attn_body(q_ref, k_ref, v_ref, qw_ref, kw_ref, *rest, heads):
    o_ref, kn_ref, vo_ref = rest[-3:]
    scale = 1.0 / math.sqrt(NA_HEAD_DIM)
    vo_ref[...] = v_ref[...]
    for hh in range(heads):
        sl = slice(hh * NA_HEAD_DIM, (hh + 1) * NA_HEAD_DIM)
        qn = _head_rms(q_ref[:, sl], qw_ref[...]) * scale
        kn = _head_rms(k_ref[:, sl], kw_ref[...])
        kn_ref[:, sl] = kn
        s = _nt_dot(qn.astype(BF16), kn.astype(BF16))
        e = jnp.exp(s - jnp.max(s, axis=-1, keepdims=True))
        o = jnp.dot(e.astype(BF16), v_ref[:, sl].astype(BF16), preferred_element_type=F32)
        o_ref[:, sl] = (o / jnp.sum(e, axis=-1, keepdims=True)).astype(o_ref.dtype)


def dense_attention(proj, q_norm_w, k_norm_w, j, n_even, caches=None):
    heads = 4
    bw = heads * NA_HEAD_DIM
    nb = NA_WIDTH // bw
    cache_spec = pl.BlockSpec((None, None, SEQ, bw), lambda s, h: (s, j, 0, h))
    cache_shape = jax.ShapeDtypeStruct((BATCH, n_even, SEQ, NA_WIDTH), F32)
    carried = () if caches is None else tuple(caches)
    return pl.pallas_call(
        functools.partial(_dense_attn_body, heads=heads),
        grid=(BATCH, nb),
        in_specs=[
            pl.BlockSpec((SEQ, bw), lambda s, h: (s, h)),
            pl.BlockSpec((SEQ, bw), lambda s, h: (s, nb + h)),
            pl.BlockSpec((SEQ, bw), lambda s, h: (s, 2 * nb + h)),
            pl.BlockSpec((1, NA_HEAD_DIM), lambda s, h: (0, 0)),
            pl.BlockSpec((1, NA_HEAD_DIM), lambda s, h: (0, 0)),
            *[pl.BlockSpec(memory_space=pl.ANY) for _ in carried],
        ],
        out_specs=[pl.BlockSpec((SEQ, bw), lambda s, h: (s, h)), cache_spec, cache_spec],
        out_shape=[jax.ShapeDtypeStruct((N_TOK, D_MODEL), BF16), cache_shape, cache_shape],
        input_output_aliases={5 + n: 1 + n for n in range(len(carried))},
        compiler_params=_cparams(("parallel", "parallel")),
        name="dense_attention",
    )(proj, proj, proj, q_norm_w.reshape(1, -1), k_norm_w.reshape(1, -1), *carried)


def _na_window_start(blk):
    return jnp.clip(NA_QROWS * blk - NA_ROWS // 2, 0, GRID_ROWS - NA_KROWS)


def _na_attn_body(q_ref, k_ref, v_ref, ck_ref, cv_ref, bias_ref, qw_ref, kw_ref, mix_ref, o_ref,
                  kn_ref, vb_ref, ckb_ref, cvb_ref):
    del mix_ref
    scale = 1.0 / math.sqrt(NA_HEAD_DIM)
    blk = pl.program_id(2)

    heads = [slice(hh * NA_HEAD_DIM, (hh + 1) * NA_HEAD_DIM) for hh in range(NA_HEADS_PER_STEP)]

    @pl.when(blk == 0)
    def _():
        for sl in heads:
            kn_ref[:, sl] = _head_rms(k_ref[:, sl], kw_ref[...]).astype(BF16)
        vb_ref[...] = v_ref[...].astype(BF16)
        ckb_ref[...] = ck_ref[...].astype(BF16)
        cvb_ref[...] = cv_ref[...].astype(BF16)

    w0 = pl.multiple_of(_na_window_start(blk) * GRID_W, GRID_W)
    nk = NA_KROWS * GRID_W
    for hh, sl in enumerate(heads):
        qn = (_head_rms(q_ref[:, sl], qw_ref[...]) * scale).astype(BF16)
        s_loc = _nt_dot(qn, kn_ref[pl.ds(w0, nk), sl]) + bias_ref[hh]
        s_ctx = _nt_dot(qn, ckb_ref[:, sl])
        m = jnp.maximum(jnp.max(s_loc, axis=-1, keepdims=True), jnp.max(s_ctx, axis=-1, keepdims=True))
        e_loc = jnp.exp(s_loc - m)
        e_ctx = jnp.exp(s_ctx - m)
        den = jnp.sum(e_loc, axis=-1, keepdims=True) + jnp.sum(e_ctx, axis=-1, keepdims=True)
        o = (jnp.dot(e_loc.astype(BF16), vb_ref[pl.ds(w0, nk), sl], preferred_element_type=F32)
             + jnp.dot(e_ctx.astype(BF16), cvb_ref[:, sl], preferred_element_type=F32))
        o_ref[:, sl] = (o / den).astype(o_ref.dtype)


def _na_bias_table(rpb):
    nh, nblk, w = rpb.shape[0], GRID_ROWS // NA_QROWS, GRID_W
    v = jnp.pad(rpb, ((0, 0), (0, 0), (w - NA_COLS, w - NA_COLS + 1)))
    toe = jnp.tile(v, (1, 1, w))[:, :, :w * (2 * w - 1)].reshape(nh, -1, w, 2 * w - 1)[..., w - 1:]
    qc, kc = np.arange(w)[:, None], np.arange(w)[None, :]
    cs = np.clip(qc - NA_COLS // 2, 0, w - NA_COLS)
    toe = jnp.where(((kc >= cs) & (kc < cs + NA_COLS))[None, None], toe, NEG)
    masked = jnp.full((nh, w, w), NEG, F32)
    tabs = []
    for blk in (0, 1, nblk - 1):
        w0 = int(np.clip(NA_QROWS * blk - NA_ROWS // 2, 0, GRID_ROWS - NA_KROWS))
        qrows = []
        for a in range(NA_QROWS):
            qr = NA_QROWS * blk + a
            rs = int(np.clip(qr - NA_ROWS // 2, 0, GRID_ROWS - NA_ROWS))
            krows = [toe[:, w0 + t - qr + NA_ROWS - 1] if rs <= w0 + t < rs + NA_ROWS else masked
                     for t in range(NA_KROWS)]
            qrows.append(jnp.stack(krows, axis=2))
        tabs.append(jnp.stack(qrows, axis=1).reshape(nh, NA_QROWS * w, NA_KROWS * w))
    return jnp.stack(tabs, axis=1)


def neighbourhood_attention(proj, cache_k4, cache_v4, j, rpb, q_norm_w, k_norm_w, mix):
    nblk = GRID_ROWS // NA_QROWS
    tq = NA_QROWS * GRID_W
    tk = NA_KROWS * GRID_W
    bias = _na_bias_table(rpb)
    qrow0 = N_PROMPT // tq
    krow0 = N_PROMPT // DEC_SEQ

    def variant(blk):
        return jnp.where(blk == 0, 0, jnp.where(blk == nblk - 1, 2, 1))

    hd = NA_HEAD_DIM
    nhs = NA_HEADS_PER_STEP
    bw = nhs * hd
    ng = NA_HEADS // nhs
    return pl.pallas_call(
        _na_attn_body,
        grid=(DEC_BATCH, ng, nblk),
        in_specs=[
            pl.BlockSpec((tq, bw), lambda b, h, t: (qrow0 + b * nblk + t, h)),
            pl.BlockSpec((DEC_SEQ, bw), lambda b, h, t: (krow0 + b, ng + h)),
            pl.BlockSpec((DEC_SEQ, bw), lambda b, h, t: (krow0 + b, 2 * ng + h)),
            pl.BlockSpec((None, None, PAST_LEN, bw), lambda b, h, t: (b, j, 0, h)),
            pl.BlockSpec((None, None, PAST_LEN, bw), lambda b, h, t: (b, j, 0, h)),
            pl.BlockSpec((nhs, None, tq, tk), lambda b, h, t: (h, variant(t), 0, 0)),
            pl.BlockSpec((1, hd), lambda b, h, t: (0, 0)),
            pl.BlockSpec((1, hd), lambda b, h, t: (0, 0)),
            pl.BlockSpec(memory_space=pl.ANY),
        ],
        out_specs=pl.BlockSpec((tq, bw), lambda b, h, t: (qrow0 + b * nblk + t, h)),
        out_shape=jax.ShapeDtypeStruct(mix.shape, mix.dtype),
        input_output_aliases={8: 0},
        scratch_shapes=[pltpu.VMEM((DEC_SEQ, bw), BF16), pltpu.VMEM((DEC_SEQ, bw), BF16),
                        pltpu.VMEM((PAST_LEN, bw), BF16), pltpu.VMEM((PAST_LEN, bw), BF16)],
        compiler_params=_cparams(("parallel", "parallel", "arbitrary")),
        name="neighbourhood_attention",
    )(proj, proj, proj, cache_k4, cache_v4, bias, q_norm_w.reshape(1, -1), k_norm_w.reshape(1, -1), mix)


def _tile_seq_pos(c, rt):
    n_p = N_PROMPT // rt
    pp, ps = SEQ // rt, DEC_SEQ // rt
    pos = jnp.where(c < n_p, c % pp, (c - n_p) % ps)
    per = jnp.where(c < n_p, pp, ps)
    return pos == 0, pos == per - 1


def _halo_specs(rt, width, col_of, halo):
    per = rt // halo
    last = N_TOK // halo - 1
    return [
        pl.BlockSpec((halo, width), lambda c, j: (jnp.maximum(c * per - 1, 0), col_of(j))),
        pl.BlockSpec((rt, width), lambda c, j: (c, col_of(j))),
        pl.BlockSpec((halo, width), lambda c, j: (jnp.minimum((c + 1) * per, last), col_of(j))),
    ]


def _ssd_conv_body(prev_ref, cur_ref, next_ref, w_ref, b_ref, o_ref, ext_ref, *, rt):
    first, last = _tile_seq_pos(pl.program_id(0), rt)
    ext_ref[0:HALO] = jnp.where(first, 0.0, prev_ref[...])
    ext_ref[HALO:HALO + rt] = cur_ref[...]
    ext_ref[HALO + rt:] = jnp.where(last, 0.0, next_ref[...])
    pad = SSD_CONV_W // 2
    ext = ext_ref[...]
    acc = b_ref[...]
    for k in range(SSD_CONV_W):
        shifted = ext if k == pad else pltpu.roll(ext, (pad - k) % ext.shape[0], 0)
        acc = acc + shifted[HALO:HALO + rt] * w_ref[k:k + 1]
    o_ref[...] = _silu(acc)


def ssd_conv(proj, conv_w, conv_b, rt=ROW_TILE, width=1024):
    col0 = (3 * NA_WIDTH + SSD_INNER) // width
    return pl.pallas_call(
        functools.partial(_ssd_conv_body, rt=rt),
        grid=(N_TOK // rt, SSD_CONV_DIM // width),
        in_specs=[
            *_halo_specs(rt, width, lambda j: col0 + j, HALO),
            pl.BlockSpec((SSD_CONV_W, width), lambda c, j: (0, j)),
            pl.BlockSpec((1, width), lambda c, j: (0, j)),
        ],
        out_specs=pl.BlockSpec((rt, width), lambda c, j: (c, j)),
        out_shape=jax.ShapeDtypeStruct((N_TOK, SSD_CONV_DIM), F32),
        scratch_shapes=[pltpu.VMEM((rt + 2 * HALO, width), F32)],
        compiler_params=_cparams(("parallel", "parallel")),
        name="ssd_conv",
    )(proj, proj, proj, conv_w, conv_b.reshape(1, -1))


def _short_conv_body(gb_ref, cp_ref, cc_ref, cn_ref, xp_ref, xc_ref, xn_ref, w_ref, o_ref, ext_ref, *, rt):
    first, last = _tile_seq_pos(pl.program_id(0), rt)

    def prod(c_ref, x_ref):
        return c_ref[...].astype(F32) * x_ref[...].astype(F32)

    h = HALO_BF16
    ext_ref[0:h] = jnp.where(first, 0.0, prod(cp_ref, xp_ref))
    ext_ref[h:h + rt] = prod(cc_ref, xc_ref)
    ext_ref[h + rt:] = jnp.where(last, 0.0, prod(cn_ref, xn_ref))
    pad = SC_CONV_W // 2
    ext = ext_ref[...]
    acc = None
    for k in range(SC_CONV_W):
        shifted = ext if k == pad else pltpu.roll(ext, (pad - k) % ext.shape[0], 0)
        term = shifted[h:h + rt] * w_ref[k:k + 1]
        acc = term if acc is None else acc + term
    o_ref[...] = (gb_ref[...].astype(F32) * acc).astype(o_ref.dtype)


def short_conv_gate(proj, conv_w, rt=ROW_TILE, width=1024):
    nb = SC_WIDTH // width
    return pl.pallas_call(
        functools.partial(_short_conv_body, rt=rt),
        grid=(N_TOK // rt, nb),
        in_specs=[
            pl.BlockSpec((rt, width), lambda c, j: (c, j)),
            *_halo_specs(rt, width, lambda j: nb + j, HALO_BF16),
            *_halo_specs(rt, width, lambda j: 2 * nb + j, HALO_BF16),
            pl.BlockSpec((SC_CONV_W, width), lambda c, j: (0, j)),
        ],
        out_specs=pl.BlockSpec((rt, width), lambda c, j: (c, j)),
        out_shape=jax.ShapeDtypeStruct((N_TOK, SC_WIDTH), BF16),
        scratch_shapes=[pltpu.VMEM((rt + 2 * HALO_BF16, width), F32)],
        compiler_params=_cparams(("parallel", "parallel")),
        name="short_conv_gate",
    )(proj, proj, proj, proj, proj, proj, proj, conv_w)


def _chunk_cumsum(x, reverse):
    rows = lax.broadcasted_iota(jnp.int32, x.shape, 0)
    n = x.shape[0]
    k = 1
    while k < n:
        if reverse:
            x = x + jnp.where(rows < n - k, pltpu.roll(x, n - k, 0), 0.0)
        else:
            x = x + jnp.where(rows >= k, pltpu.roll(x, k, 0), 0.0)
        k *= 2
    return x


def _dt_and_logdecay(dt_ref, bias_ref, a_ref):
    dtv = _softplus(dt_ref[...] + bias_ref[...])
    dta = dtv * a_ref[...]
    lane = lax.broadcasted_iota(jnp.int32, dta.shape, 1)
    acum = jnp.where(lane < SSD_HEADS, _chunk_cumsum(dta, False), _chunk_cumsum(dta, True))
    return dtv, acum


def _pair_cols(m, c0, lane):
    shape = (m.shape[0], LANES)
    return jnp.where(lane[:m.shape[0]] < SSD_HEAD_DIM,
                     jnp.broadcast_to(m[:, c0:c0 + 1], shape),
                     jnp.broadcast_to(m[:, c0 + 1:c0 + 2], shape))


def _ssd_state_body(xf_ref, bf_ref, dtf_ref, xb_ref, bb_ref, dtb_ref, h0_ref, bias_ref, a_ref, *rest, nc, final):
    h_ref = rest[-1]
    hpf_ref, hpb_ref = rest[-4:-2] if final else rest[-3:-1]
    t = pl.program_id(1)

    @pl.when(t == 0)
    def _():
        h_ref[...] = h0_ref[...]

    lane = lax.broadcasted_iota(jnp.int32, (SSD_CHUNK, LANES), 1)
    for d, (x_ref, b_ref, dt_ref, hp_ref) in enumerate(
            ((xf_ref, bf_ref, dtf_ref, hpf_ref), (xb_ref, bb_ref, dtb_ref, hpb_ref))):
        dtv, acum = _dt_and_logdecay(dt_ref, bias_ref, a_ref)
        edge = acum[SSD_CHUNK - 1:SSD_CHUNK] if d == 0 else acum[0:1]
        s = dtv * jnp.exp(edge - acum)
        cd = jnp.exp(edge)
        hp_ref[...] = h_ref[d].astype(BF16)
        for g in range(SSD_GROUPS):
            bt = jnp.transpose(b_ref[:, g * SSD_STATE:(g + 1) * SSD_STATE]).astype(BF16)
            for pr in range(SSD_HPG // 2):
                head = g * SSD_HPG + 2 * pr
                c0 = d * SSD_HEADS + head
                cols = slice(head * SSD_HEAD_DIM, head * SSD_HEAD_DIM + LANES)
                xd = (x_ref[:, cols] * _pair_cols(s, c0, lane)).astype(BF16)
                st = jnp.dot(bt, xd, preferred_element_type=F32)
                h_ref[d, :, cols] = h_ref[d, :, cols] * _pair_cols(cd, c0, lane) + st

    if final:
        hfin_ref = rest[-2]

        @pl.when(t == nc - 1)
        def _():
            for d in range(2):
                for c in range(0, SSD_INNER, LANES):
                    hfin_ref[d, c:c + LANES, :] = jnp.transpose(h_ref[d, :, c:c + LANES])


def ssd_states(xbc, dt_raw, h0t, dt_bias, a_neg, row0, nseq, nc, final=None):
    c0 = row0 // SSD_CHUNK
    hp = SSD_INNER
    bcol = SSD_INNER // (SSD_GROUPS * SSD_STATE)
    out_specs = [
        pl.BlockSpec((None, None, SSD_STATE, hp), lambda s, t: (s, t, 0, 0)),
        pl.BlockSpec((None, None, SSD_STATE, hp), lambda s, t: (s, nc - 1 - t, 0, 0)),
    ]
    out_shape = [jax.ShapeDtypeStruct((nseq, nc, SSD_STATE, hp), BF16)] * 2
    carried, aliases = (), {}
    if final is not None:
        j, n_even, earlier = final
        out_specs.append(pl.BlockSpec((None, None, 2, hp, SSD_STATE), lambda s, t: (s, j, 0, 0, 0)))
        out_shape.append(jax.ShapeDtypeStruct((nseq, n_even, 2, hp, SSD_STATE), F32))
        if earlier is not None:
            carried, aliases = (earlier,), {9: 2}

    def fwd(s, t):
        return c0 + s * nc + t

    def bwd(s, t):
        return c0 + s * nc + (nc - 1 - t)

    return pl.pallas_call(
        functools.partial(_ssd_state_body, nc=nc, final=final is not None),
        grid=(nseq, nc),
        in_specs=[
            pl.BlockSpec((SSD_CHUNK, hp), lambda s, t: (fwd(s, t), 0)),
            pl.BlockSpec((SSD_CHUNK, SSD_GROUPS * SSD_STATE), lambda s, t: (fwd(s, t), bcol)),
            pl.BlockSpec((SSD_CHUNK, LANES), lambda s, t: (fwd(s, t), 0)),
            pl.BlockSpec((SSD_CHUNK, hp), lambda s, t: (bwd(s, t), 0)),
            pl.BlockSpec((SSD_CHUNK, SSD_GROUPS * SSD_STATE), lambda s, t: (bwd(s, t), bcol)),
            pl.BlockSpec((SSD_CHUNK, LANES), lambda s, t: (bwd(s, t), 0)),
            pl.BlockSpec((None, 2, SSD_STATE, hp), lambda s, t: (s, 0, 0, 0)),
            pl.BlockSpec((1, LANES), lambda s, t: (0, 0)),
            pl.BlockSpec((1, LANES), lambda s, t: (0, 0)),
            *[pl.BlockSpec(memory_space=pl.ANY) for _ in carried],
        ],
        out_specs=out_specs,
        out_shape=out_shape,
        input_output_aliases=aliases,
        scratch_shapes=[pltpu.VMEM((2, SSD_STATE, hp), F32)],
        compiler_params=_cparams(("parallel", "arbitrary")),
        name="ssd_states",
    )(xbc, xbc, dt_raw, xbc, xbc, dt_raw, h0t, dt_bias, a_neg, *carried)


def _ssd_out_body(x_ref, b_ref, c_ref, z_ref, dt_ref, hpf_ref, hpb_ref, bias_ref, a_ref, dsk_ref, nw_ref,
                  mix_ref, o_ref, g_ref):
    del mix_ref
    dtv, acum = _dt_and_logdecay(dt_ref, bias_ref, a_ref)
    acum_t = jnp.transpose(acum)
    dt_t = jnp.transpose(dtv)
    shape = (SSD_CHUNK, SSD_CHUNK)
    rows = lax.broadcasted_iota(jnp.int32, shape, 0)
    lane = lax.broadcasted_iota(jnp.int32, shape, 1)
    lower = rows >= lane
    upper = rows <= lane

    def head_terms(cb, head):
        cf, cbk = head, SSD_HEADS + head
        col_f = jnp.broadcast_to(acum[:, cf:cf + 1], shape)
        col_b = jnp.broadcast_to(acum[:, cbk:cbk + 1], shape)
        lf = jnp.exp(jnp.where(lower, col_f - acum_t[cf:cf + 1, :], -jnp.inf)) * dt_t[cf:cf + 1, :]
        lb = jnp.exp(jnp.where(upper, col_b - acum_t[cbk:cbk + 1, :], -jnp.inf)) * dt_t[cbk:cbk + 1, :]
        return (cb * (lf + lb)).astype(BF16), jnp.exp(col_f), jnp.exp(col_b)

    for g in range(SSD_GROUPS):
        gcols = slice(g * SSD_GROUP_W, (g + 1) * SSD_GROUP_W)
        bg = b_ref[:, g * SSD_STATE:(g + 1) * SSD_STATE].astype(BF16)
        cg = c_ref[:, g * SSD_STATE:(g + 1) * SSD_STATE].astype(BF16)
        cb = _nt_dot(cg, bg)
        yoff_f = jnp.dot(cg, hpf_ref[:, gcols], preferred_element_type=F32)
        yoff_b = jnp.dot(cg, hpb_ref[:, gcols], preferred_element_type=F32)
        for pr in range(SSD_HPG // 2):
            head = g * SSD_HPG + 2 * pr
            cols = slice(head * SSD_HEAD_DIM, head * SSD_HEAD_DIM + LANES)
            pcols = slice(pr * LANES, (pr + 1) * LANES)
            xp = x_ref[:, cols]
            xb = xp.astype(BF16)
            first = lane < SSD_HEAD_DIM
            w0, ef0, eb0 = head_terms(cb, head)
            w1, ef1, eb1 = head_terms(cb, head + 1)
            y = jnp.where(first, jnp.dot(w0, xb, preferred_element_type=F32),
                          jnp.dot(w1, xb, preferred_element_type=F32))
            y = y + jnp.where(first, ef0, ef1) * yoff_f[:, pcols]
            y = y + jnp.where(first, eb0, eb1) * yoff_b[:, pcols]
            y = y + xp * dsk_ref[:, cols]
            g_ref[:, cols] = y * _silu(z_ref[:, cols])
        gg = g_ref[:, gcols]
        ms = jnp.mean(gg * gg, axis=-1, keepdims=True)
        o_ref[:, gcols] = (gg * lax.rsqrt(ms + EPS) * nw_ref[:, gcols]).astype(o_ref.dtype)


def ssd_outputs(xbc, proj, dt_raw, hpf, hpb, dt_bias, a_neg, d_skip, norm_w, row0, nseq, nc, mix):
    c0 = row0 // SSD_CHUNK
    hp = SSD_INNER
    gw = SSD_GROUPS * SSD_STATE
    bcol = SSD_INNER // gw
    zcol = 3 * NA_WIDTH // SSD_INNER

    def row(s, t):
        return c0 + s * nc + t

    return pl.pallas_call(
        _ssd_out_body,
        grid=(nseq, nc),
        in_specs=[
            pl.BlockSpec((SSD_CHUNK, hp), lambda s, t: (row(s, t), 0)),
            pl.BlockSpec((SSD_CHUNK, gw), lambda s, t: (row(s, t), bcol)),
            pl.BlockSpec((SSD_CHUNK, gw), lambda s, t: (row(s, t), bcol + 1)),
            pl.BlockSpec((SSD_CHUNK, hp), lambda s, t: (row(s, t), zcol)),
            pl.BlockSpec((SSD_CHUNK, LANES), lambda s, t: (row(s, t), 0)),
            pl.BlockSpec((None, None, SSD_STATE, hp), lambda s, t: (s, t, 0, 0)),
            pl.BlockSpec((None, None, SSD_STATE, hp), lambda s, t: (s, t, 0, 0)),
            pl.BlockSpec((1, LANES), lambda s, t: (0, 0)),
            pl.BlockSpec((1, LANES), lambda s, t: (0, 0)),
            pl.BlockSpec((1, hp), lambda s, t: (0, 0)),
            pl.BlockSpec((1, hp), lambda s, t: (0, 0)),
            pl.BlockSpec(memory_space=pl.ANY),
        ],
        out_specs=pl.BlockSpec((SSD_CHUNK, hp), lambda s, t: (row(s, t), NA_WIDTH // hp)),
        out_shape=jax.ShapeDtypeStruct(mix.shape, mix.dtype),
        input_output_aliases={11: 0},
        scratch_shapes=[pltpu.VMEM((SSD_CHUNK, hp), F32)],
        compiler_params=_cparams(("parallel", "parallel")),
        name="ssd_outputs",
    )(xbc, xbc, xbc, proj, dt_raw, hpf, hpb, dt_bias, a_neg, d_skip, norm_w, mix)


def _pad_lanes(v):
    return jnp.pad(v.reshape(1, -1), ((0, 0), (0, LANES - v.size)))


def ssd_mixer(proj, dt_raw, state_h0, conv_w, conv_b, a_log, dt_bias, d_skip, norm_w, mix, final):
    xbc = ssd_conv(proj, conv_w, conv_b)
    bias = _pad_lanes(dt_bias)
    a_neg = _pad_lanes(-jnp.exp(a_log))
    dsk = jnp.repeat(d_skip, SSD_HEAD_DIM).reshape(1, -1)
    nw = norm_w.reshape(1, -1)

    def to_t(h):
        return jnp.transpose(h, (0, 1, 4, 2, 3)).reshape(h.shape[0], 2, SSD_STATE, SSD_INNER)

    nc_p, nc_s = SEQ // SSD_CHUNK, DEC_SEQ // SSD_CHUNK
    h0_p = jnp.zeros((BATCH, 2, SSD_STATE, SSD_INNER), F32)
    hpf, hpb, hfin = ssd_states(xbc, dt_raw, h0_p, bias, a_neg, 0, BATCH, nc_p, final)
    mix = ssd_outputs(xbc, proj, dt_raw, hpf, hpb, bias, a_neg, dsk, nw, 0, BATCH, nc_p, mix)
    hpf, hpb = ssd_states(xbc, dt_raw, to_t(state_h0), bias, a_neg, N_PROMPT, DEC_BATCH, nc_s)
    mix = ssd_outputs(xbc, proj, dt_raw, hpf, hpb, bias, a_neg, dsk, nw, N_PROMPT, DEC_BATCH, nc_s, mix)
    return mix, hfin


def kernel(x_prompt, x_sample, cache_k, cache_v, state_ssm, c, c_ctx, norm1_w, norm2_w, w_mod, b_mod,
           w_in_even, w_out_even, q_norm_w, k_norm_w, na_rel_bias, ssd_conv_w, ssd_conv_b, ssd_a_log,
           ssd_dt_bias, ssd_d, ssd_norm_w, w_in_odd, sc_conv_w, w_out_odd, w_ffn_gate, w_ffn_up,
           w_ffn_down):
    d = D_MODEL
    x = (x_prompt.reshape(N_PROMPT, d), x_sample.reshape(N_SAMPLE, d))
    cond8 = jnp.concatenate([c_ctx[None, :], c, jnp.zeros((8 - 1 - DEC_BATCH, d), F32)], axis=0)
    mods = adaln_table(cond8, w_mod, b_mod)
    n_even = w_in_even.shape[0]
    cache_k4 = cache_k.reshape(DEC_BATCH, n_even, PAST_LEN, NA_WIDTH)
    cache_v4 = cache_v.reshape(DEC_BATCH, n_even, PAST_LEN, NA_WIDTH)

    w_down = cast_weights_bf16(w_ffn_down)
    caches, states = None, None
    for l in range(DEPTH):
        j = l // 2
        h = modnorm(x, norm1_w[l], mods, l, 0)
        if l % 2 == 0:
            w_dt = jnp.pad(w_in_even[j:j + 1, :, EVEN_MAIN:], ((0, 0), (0, 0), (0, LANES - 2 * SSD_HEADS)))
            pr = proj(h, w_in_even, j, EVEN_MAIN, F32)
            dt_raw = proj(h, w_dt, 0, LANES, F32, tn=LANES)
            mix, *caches = dense_attention(pr, q_norm_w[j], k_norm_w[j], j, n_even, caches)
            mix = neighbourhood_attention(pr, cache_k4, cache_v4, j, na_rel_bias[j],
                                          q_norm_w[j], k_norm_w[j], mix)
            mix, states = ssd_mixer(pr, dt_raw, state_ssm[:, j], ssd_conv_w[j], ssd_conv_b[j],
                                    ssd_a_log[j].reshape(-1), ssd_dt_bias[j].reshape(-1), ssd_d[j],
                                    ssd_norm_w[j], mix, (j, n_even, states))
            x_mid = mixer_out_proj(mix, w_out_even, j, x, mods, l)
        else:
            pr = proj(h, w_in_odd, j, 3 * SC_WIDTH, BF16)
            mix = short_conv_gate(pr, sc_conv_w[j])
            x_mid = mixer_out_proj(mix, w_out_odd, j, x, mods, l)
        h = modnorm((x_mid,), norm2_w[l], mods, l, 3)
        hidden = swiglu(h, w_ffn_gate, w_ffn_up, l)
        if l < DEPTH - 1:
            x = (ffn_down_proj(hidden, w_down, x_mid, mods, l),)
        else:
            y_prompt = ffn_down_proj(hidden, w_down, x_mid, mods, l, 0, N_PROMPT).reshape(BATCH, SEQ, d)
            y_sample = ffn_down_proj(hidden, w_down, x_mid, mods, l, N_PROMPT, N_SAMPLE).reshape(DEC_BATCH, DEC_SEQ, d)

    new_k, new_v = (t.reshape(BATCH, n_even, SEQ, NA_HEADS, NA_HEAD_DIM) for t in caches)
    new_h = states.reshape(BATCH, n_even, 2, SSD_HEADS, SSD_HEAD_DIM, SSD_STATE)
    return (y_prompt, y_sample, new_k, new_v, new_h)
```

```python
import functools
import math

import numpy as np
import jax
import jax.numpy as jnp
from jax import lax
from jax.experimental import pallas as pl
from jax.experimental.pallas import tpu as pltpu

D_MODEL = 4096
BATCH, SEQ = 32, 256
DEC_BATCH, DEC_SEQ = 4, 4096
PAST_LEN = 256
DEPTH = 4
GRID_W = 64
GRID_ROWS = DEC_SEQ // GRID_W
NA_HEADS, NA_HEAD_DIM = 16, 128
NA_WIDTH = NA_HEADS * NA_HEAD_DIM
NA_ROWS, NA_COLS = 8, 16
SSD_HEAD_DIM = 64
SSD_INNER = D_MODEL // 2
SSD_HEADS = SSD_INNER // SSD_HEAD_DIM
SSD_GROUPS = 4
SSD_HPG = SSD_HEADS // SSD_GROUPS
SSD_STATE = 128
SSD_CONV_W = 5
SSD_CHUNK = 128
SSD_CONV_DIM = SSD_INNER + 2 * SSD_GROUPS * SSD_STATE
SSD_GROUP_W = SSD_HPG * SSD_HEAD_DIM
EVEN_MAIN = 3 * NA_WIDTH + SSD_INNER + SSD_CONV_DIM
SC_WIDTH = D_MODEL
SC_CONV_W = 3
EPS = 1e-6

N_PROMPT = BATCH * SEQ
N_SAMPLE = DEC_BATCH * DEC_SEQ
N_TOK = N_PROMPT + N_SAMPLE

LANES = 128
SUBLANES = 8
VMEM_LIMIT = 56 * 1024 * 1024
ROW_TILE = 256
HALO = SUBLANES
HALO_BF16 = 2 * SUBLANES
NA_QROWS = 4
NA_KROWS = NA_QROWS + NA_ROWS
NA_HEADS_PER_STEP = 4
NEG = -1e30

F32 = jnp.float32
BF16 = jnp.bfloat16


def _cparams(sem):
    return pltpu.CompilerParams(dimension_semantics=sem, vmem_limit_bytes=VMEM_LIMIT)


def _silu(x):
    return x / (1.0 + jnp.exp(-x))


def _softplus(x):
    return jnp.maximum(x, 0.0) + jnp.log(1.0 + jnp.exp(-jnp.abs(x)))


def _cond_row(i, tm):
    n_p = N_PROMPT // tm
    per = DEC_SEQ // tm
    return jnp.where(i < n_p, 0, 1 + (i - n_p) // per)


def _adaln_body(c_ref, w_ref, b_ref, o_ref):
    a = _silu(c_ref[...]).astype(BF16)
    o_ref[...] = jnp.dot(a, w_ref[...].astype(BF16), preferred_element_type=F32) + b_ref[...]


def adaln_table(cond8, w_mod, b_mod):
    depth, d, n6 = w_mod.shape
    tn = 512
    out = pl.pallas_call(
        _adaln_body,
        grid=(depth, n6 // tn),
        in_specs=[
            pl.BlockSpec((8, d), lambda l, j: (0, 0)),
            pl.BlockSpec((None, d, tn), lambda l, j: (l, 0, j)),
            pl.BlockSpec((None, 1, tn), lambda l, j: (l, 0, j)),
        ],
        out_specs=pl.BlockSpec((None, 8, tn), lambda l, j: (l, 0, j)),
        out_shape=jax.ShapeDtypeStruct((depth, 8, n6), F32),
        compiler_params=_cparams(("parallel", "parallel")),
        name="adaln_table",
    )(cond8, w_mod, b_mod.reshape(depth, 1, n6))
    return out.reshape(depth, 8, 6, 1, d)


def _row_part_specs(parts, tm, tn, ids):
    if len(parts) == 1:
        return [pl.BlockSpec((tm, tn), lambda *g: ids(*g))]
    n_p = N_PROMPT // tm
    return [
        pl.BlockSpec((tm, tn), lambda *g: (jnp.minimum(ids(*g)[0], n_p - 1), ids(*g)[1])),
        pl.BlockSpec((tm, tn), lambda *g: (jnp.maximum(ids(*g)[0] - n_p, 0), ids(*g)[1])),
    ]


def _modnorm_body(*refs, tm):
    *x_refs, g_ref, sh_ref, sc_ref, h_ref, gain_ref = refs
    gain_ref[...] = g_ref[...] * (1.0 + sc_ref[...])
    d = h_ref.shape[1]
    rows = 8 * SUBLANES
    chunk = 2 * LANES

    def run(x_ref):
        def body(r, carry):
            sl = pl.ds(pl.multiple_of(r * rows, rows), rows)
            acc = jnp.zeros((rows, LANES), F32)
            for c in range(0, d, LANES):
                xc = x_ref[sl, c:c + LANES]
                acc = acc + xc * xc
            rinv = lax.rsqrt(jnp.sum(acc, axis=-1, keepdims=True) / d + EPS)
            for c in range(0, d, chunk):
                cs = slice(c, c + chunk)
                h_ref[sl, cs] = (x_ref[sl, cs] * rinv * gain_ref[:, cs] + sh_ref[:, cs]).astype(h_ref.dtype)
            return carry

        lax.fori_loop(0, tm // rows, body, 0)

    if len(x_refs) == 1:
        run(x_refs[0])
    else:
        in_prompt = pl.program_id(0) < N_PROMPT // tm
        pl.when(in_prompt)(lambda: run(x_refs[0]))
        pl.when(jnp.logical_not(in_prompt))(lambda: run(x_refs[1]))


def modnorm(x_parts, norm_w, mods, layer, which_shift, tm=2 * ROW_TILE):
    d = D_MODEL

    def mod_spec(which):
        return pl.BlockSpec((None, None, None, 1, d), lambda i: (layer, _cond_row(i, tm), which, 0, 0))

    return pl.pallas_call(
        functools.partial(_modnorm_body, tm=tm),
        grid=(N_TOK // tm,),
        in_specs=[
            *_row_part_specs(x_parts, tm, d, lambda i: (i, 0)),
            pl.BlockSpec((1, d), lambda i: (0, 0)),
            mod_spec(which_shift),
            mod_spec(which_shift + 1),
        ],
        out_specs=pl.BlockSpec((tm, d), lambda i: (i, 0)),
        out_shape=jax.ShapeDtypeStruct((N_TOK, d), BF16),
        scratch_shapes=[pltpu.VMEM((1, d), F32)],
        compiler_params=_cparams(("parallel",)),
        name="modnorm",
    )(*x_parts, norm_w.reshape(1, d), mods, mods)


def _cast_weight_once(w_ref, wb_ref):
    @pl.when(pl.program_id(1) == 0)
    def _():
        wb_ref[...] = w_ref[...].astype(BF16)


def _proj_body(h_ref, w_ref, o_ref, wb_ref):
    _cast_weight_once(w_ref, wb_ref)
    o_ref[...] = jnp.dot(h_ref[...], wb_ref[...], preferred_element_type=F32).astype(o_ref.dtype)


def _weight_spec(kdim, tn, layer):
    return pl.BlockSpec((None, kdim, tn), lambda j, i: (layer, 0, j), pipeline_mode=pl.Buffered(1))


def proj(h, w3, widx, n, out_dtype, tn=1024, tm=1024):
    m, d = h.shape
    return pl.pallas_call(
        _proj_body,
        grid=(pl.cdiv(n, tn), m // tm),
        in_specs=[
            pl.BlockSpec((tm, d), lambda j, i: (i, 0)),
            _weight_spec(d, tn, widx),
        ],
        out_specs=pl.BlockSpec((tm, tn), lambda j, i: (i, j)),
        out_shape=jax.ShapeDtypeStruct((m, n), out_dtype),
        scratch_shapes=[pltpu.VMEM((d, tn), BF16)],
        compiler_params=_cparams(("parallel", "arbitrary")),
        name="proj",
    )(h, w3)


def _swiglu_body(h_ref, wg_ref, wu_ref, o_ref, wgb_ref, wub_ref):
    _cast_weight_once(wg_ref, wgb_ref)
    _cast_weight_once(wu_ref, wub_ref)
    h = h_ref[...]
    a = jnp.dot(h, wgb_ref[...], preferred_element_type=F32)
    b = jnp.dot(h, wub_ref[...], preferred_element_type=F32)
    o_ref[...] = (_silu(a) * b).astype(o_ref.dtype)


def swiglu(h, wg3, wu3, layer, tn=512, tm=1024):
    m, d = h.shape
    n = wg3.shape[2]
    wspec = _weight_spec(d, tn, layer)
    return pl.pallas_call(
        _swiglu_body,
        grid=(pl.cdiv(n, tn), m // tm),
        in_specs=[pl.BlockSpec((tm, d), lambda j, i: (i, 0)), wspec, wspec],
        out_specs=pl.BlockSpec((tm, tn), lambda j, i: (i, j)),
        out_shape=jax.ShapeDtypeStruct((m, n), BF16),
        scratch_shapes=[pltpu.VMEM((d, tn), BF16), pltpu.VMEM((d, tn), BF16)],
        compiler_params=_cparams(("parallel", "arbitrary")),
        name="swiglu",
    )(h, wg3, wu3)


def _mixer_out_body(a_ref, w_ref, g_ref, *refs, tm):
    *x_refs, o_ref, wb_ref = refs
    _cast_weight_once(w_ref, wb_ref)
    if len(x_refs) == 1:
        x = x_refs[0][...]
    else:
        x = jnp.where(pl.program_id(1) < N_PROMPT // tm, x_refs[0][...], x_refs[1][...])
    o_ref[...] = x + g_ref[...] * jnp.dot(a_ref[...], wb_ref[...], preferred_element_type=F32)


def mixer_out_proj(a, w3, widx, x_parts, mods, layer, tn=1024, tm=512):
    m, kdim = a.shape
    n = w3.shape[2]
    return pl.pallas_call(
        functools.partial(_mixer_out_body, tm=tm),
        grid=(n // tn, m // tm),
        in_specs=[
            pl.BlockSpec((tm, kdim), lambda j, i: (i, 0)),
            _weight_spec(kdim, tn, widx),
            pl.BlockSpec((None, None, None, 1, tn), lambda j, i: (layer, _cond_row(i, tm), 2, 0, j)),
            *_row_part_specs(x_parts, tm, tn, lambda j, i: (i, j)),
        ],
        out_specs=pl.BlockSpec((tm, tn), lambda j, i: (i, j)),
        out_shape=jax.ShapeDtypeStruct((m, n), F32),
        scratch_shapes=[pltpu.VMEM((kdim, tn), BF16)],
        compiler_params=_cparams(("parallel", "arbitrary")),
        name="mixer_out_proj",
    )(a, w3, mods, *x_parts)


def _cast_body(w_ref, o_ref):
    o_ref[...] = w_ref[...].astype(o_ref.dtype)


def cast_weights_bf16(w3, tk=512):
    layers, kdim, n = w3.shape
    spec = pl.BlockSpec((None, tk, n), lambda l, k: (l, k, 0))
    return pl.pallas_call(
        _cast_body,
        grid=(layers, pl.cdiv(kdim, tk)),
        in_specs=[spec],
        out_specs=spec,
        out_shape=jax.ShapeDtypeStruct(w3.shape, BF16),
        compiler_params=_cparams(("parallel", "parallel")),
        name="cast_weights_bf16",
    )(w3)


def _ffn_down_body(a_ref, w_ref, x_ref, g_ref, o_ref):
    o_ref[...] = x_ref[...] + g_ref[...] * jnp.dot(a_ref[...], w_ref[...], preferred_element_type=F32)


def ffn_down_proj(a, w3, x, mods, layer, row0=0, nrows=None, tn=512, tm=512):
    kdim = a.shape[1]
    n = w3.shape[2]
    nrows = a.shape[0] if nrows is None else nrows
    i0 = row0 // tm
    return pl.pallas_call(
        _ffn_down_body,
        grid=(nrows // tm, n // tn),
        in_specs=[
            pl.BlockSpec((tm, kdim), lambda i, j: (i0 + i, 0)),
            pl.BlockSpec((None, kdim, tn), lambda i, j: (layer, 0, j)),
            pl.BlockSpec((tm, tn), lambda i, j: (i0 + i, j)),
            pl.BlockSpec((None, None, None, 1, tn), lambda i, j: (layer, _cond_row(i0 + i, tm), 5, 0, j)),
        ],
        out_specs=pl.BlockSpec((tm, tn), lambda i, j: (i, j)),
        out_shape=jax.ShapeDtypeStruct((nrows, n), F32),
        compiler_params=_cparams(("parallel", "parallel")),
        name="ffn_down_proj",
    )(a, w3, x, mods)


def _head_rms(x, w):
    ms = jnp.mean(x * x, axis=-1, keepdims=True)
    return x * lax.rsqrt(ms + EPS) * w


def _nt_dot(a, b):
    return lax.dot_general(a, b, (((1,), (1,)), ((), ())), preferred_element_type=F32)


def _dense_attn_body(q_ref, k_ref, v_ref, qw_ref, kw_ref, *rest, heads):
    o_ref, kn_ref, vo_ref = rest[-3:]
    scale = 1.0 / math.sqrt(NA_HEAD_DIM)
    vo_ref[...] = v_ref[...]
    for hh in range(heads):
        sl = slice(hh * NA_HEAD_DIM, (hh + 1) * NA_HEAD_DIM)
        qn = _head_rms(q_ref[:, sl], qw_ref[...]) * scale
        kn = _head_rms(k_ref[:, sl], kw_ref[...])
        kn_ref[:, sl] = kn
        s = _nt_dot(qn.astype(BF16), kn.astype(BF16))
        e = jnp.exp(s - jnp.max(s, axis=-1, keepdims=True))
        o = jnp.dot(e.astype(BF16), v_ref[:, sl].astype(BF16), preferred_element_type=F32)
        o_ref[:, sl] = (o / jnp.sum(e, axis=-1, keepdims=True)).astype(o_ref.dtype)


def dense_attention(proj, q_norm_w, k_norm_w, j, n_even, caches=None):
    heads = 4
    bw = heads * NA_HEAD_DIM
    nb = NA_WIDTH // bw
    cache_spec = pl.BlockSpec((None, None, SEQ, bw), lambda s, h: (s, j, 0, h))
    cache_shape = jax.ShapeDtypeStruct((BATCH, n_even, SEQ, NA_WIDTH), F32)
    carried = () if caches is None else tuple(caches)
    return pl.pallas_call(
        functools.partial(_dense_attn_body, heads=heads),
        grid=(BATCH, nb),
        in_specs=[
            pl.BlockSpec((SEQ, bw), lambda s, h: (s, h)),
            pl.BlockSpec((SEQ, bw), lambda s, h: (s, nb + h)),
            pl.BlockSpec((SEQ, bw), lambda s, h: (s, 2 * nb + h)),
            pl.BlockSpec((1, NA_HEAD_DIM), lambda s, h: (0, 0)),
            pl.BlockSpec((1, NA_HEAD_DIM), lambda s, h: (0, 0)),
            *[pl.BlockSpec(memory_space=pl.ANY) for _ in carried],
        ],
        out_specs=[pl.BlockSpec((SEQ, bw), lambda s, h: (s, h)), cache_spec, cache_spec],
        out_shape=[jax.ShapeDtypeStruct((N_TOK, D_MODEL), BF16), cache_shape, cache_shape],
        input_output_aliases={5 + n: 1 + n for n in range(len(carried))},
        compiler_params=_cparams(("parallel", "parallel")),
        name="dense_attention",
    )(proj, proj, proj, q_norm_w.reshape(1, -1), k_norm_w.reshape(1, -1), *carried)


def _na_window_start(blk):
    return jnp.clip(NA_QROWS * blk - NA_ROWS // 2, 0, GRID_ROWS - NA_KROWS)


def _na_attn_body(q_ref, k_ref, v_ref, ck_ref, cv_ref, bias_ref, qw_ref, kw_ref, mix_ref, o_ref,
                  kn_ref, vb_ref, ckb_ref, cvb_ref):
    del mix_ref
    scale = 1.0 / math.sqrt(NA_HEAD_DIM)
    blk = pl.program_id(2)

    heads = [slice(hh * NA_HEAD_DIM, (hh + 1) * NA_HEAD_DIM) for hh in range(NA_HEADS_PER_STEP)]

    @pl.when(blk == 0)
    def _():
        for sl in heads:
            kn_ref[:, sl] = _head_rms(k_ref[:, sl], kw_ref[...]).astype(BF16)
        vb_ref[...] = v_ref[...].astype(BF16)
        ckb_ref[...] = ck_ref[...].astype(BF16)
        cvb_ref[...] = cv_ref[...].astype(BF16)

    w0 = pl.multiple_of(_na_window_start(blk) * GRID_W, GRID_W)
    nk = NA_KROWS * GRID_W
    for hh, sl in enumerate(heads):
        qn = (_head_rms(q_ref[:, sl], qw_ref[...]) * scale).astype(BF16)
        s_loc = _nt_dot(qn, kn_ref[pl.ds(w0, nk), sl]) + bias_ref[hh]
        s_ctx = _nt_dot(qn, ckb_ref[:, sl])
        m = jnp.maximum(jnp.max(s_loc, axis=-1, keepdims=True), jnp.max(s_ctx, axis=-1, keepdims=True))
        e_loc = jnp.exp(s_loc - m)
        e_ctx = jnp.exp(s_ctx - m)
        den = jnp.sum(e_loc, axis=-1, keepdims=True) + jnp.sum(e_ctx, axis=-1, keepdims=True)
        o = (jnp.dot(e_loc.astype(BF16), vb_ref[pl.ds(w0, nk), sl], preferred_element_type=F32)
             + jnp.dot(e_ctx.astype(BF16), cvb_ref[:, sl], preferred_element_type=F32))
        o_ref[:, sl] = (o / den).astype(o_ref.dtype)


def _na_bias_table(rpb):
    nh, nblk, w = rpb.shape[0], GRID_ROWS // NA_QROWS, GRID_W
    v = jnp.pad(rpb, ((0, 0), (0, 0), (w - NA_COLS, w - NA_COLS + 1)))
    toe = jnp.tile(v, (1, 1, w))[:, :, :w * (2 * w - 1)].reshape(nh, -1, w, 2 * w - 1)[..., w - 1:]
    qc, kc = np.arange(w)[:, None], np.arange(w)[None, :]
    cs = np.clip(qc - NA_COLS // 2, 0, w - NA_COLS)
    toe = jnp.where(((kc >= cs) & (kc < cs + NA_COLS))[None, None], toe, NEG)
    masked = jnp.full((nh, w, w), NEG, F32)
    tabs = []
    for blk in (0, 1, nblk - 1):
        w0 = int(np.clip(NA_QROWS * blk - NA_ROWS // 2, 0, GRID_ROWS - NA_KROWS))
        qrows = []
        for a in range(NA_QROWS):
            qr = NA_QROWS * blk + a
            rs = int(np.clip(qr - NA_ROWS // 2, 0, GRID_ROWS - NA_ROWS))
            krows = [toe[:, w0 + t - qr + NA_ROWS - 1] if rs <= w0 + t < rs + NA_ROWS else masked
                     for t in range(NA_KROWS)]
            qrows.append(jnp.stack(krows, axis=2))
        tabs.append(jnp.stack(qrows, axis=1).reshape(nh, NA_QROWS * w, NA_KROWS * w))
    return jnp.stack(tabs, axis=1)


def neighbourhood_attention(proj, cache_k4, cache_v4, j, rpb, q_norm_w, k_norm_w, mix):
    nblk = GRID_ROWS // NA_QROWS
    tq = NA_QROWS * GRID_W
    tk = NA_KROWS * GRID_W
    bias = _na_bias_table(rpb)
    qrow0 = N_PROMPT // tq
    krow0 = N_PROMPT // DEC_SEQ

    def variant(blk):
        return jnp.where(blk == 0, 0, jnp.where(blk == nblk - 1, 2, 1))

    hd = NA_HEAD_DIM
    nhs = NA_HEADS_PER_STEP
    bw = nhs * hd
    ng = NA_HEADS // nhs
    return pl.pallas_call(
        _na_attn_body,
        grid=(DEC_BATCH, ng, nblk),
        in_specs=[
            pl.BlockSpec((tq, bw), lambda b, h, t: (qrow0 + b * nblk + t, h)),
            pl.BlockSpec((DEC_SEQ, bw), lambda b, h, t: (krow0 + b, ng + h)),
            pl.BlockSpec((DEC_SEQ, bw), lambda b, h, t: (krow0 + b, 2 * ng + h)),
            pl.BlockSpec((None, None, PAST_LEN, bw), lambda b, h, t: (b, j, 0, h)),
            pl.BlockSpec((None, None, PAST_LEN, bw), lambda b, h, t: (b, j, 0, h)),
            pl.BlockSpec((nhs, None, tq, tk), lambda b, h, t: (h, variant(t), 0, 0)),
            pl.BlockSpec((1, hd), lambda b, h, t: (0, 0)),
            pl.BlockSpec((1, hd), lambda b, h, t: (0, 0)),
            pl.BlockSpec(memory_space=pl.ANY),
        ],
        out_specs=pl.BlockSpec((tq, bw), lambda b, h, t: (qrow0 + b * nblk + t, h)),
        out_shape=jax.ShapeDtypeStruct(mix.shape, mix.dtype),
        input_output_aliases={8: 0},
        scratch_shapes=[pltpu.VMEM((DEC_SEQ, bw), BF16), pltpu.VMEM((DEC_SEQ, bw), BF16),
                        pltpu.VMEM((PAST_LEN, bw), BF16), pltpu.VMEM((PAST_LEN, bw), BF16)],
        compiler_params=_cparams(("parallel", "parallel", "arbitrary")),
        name="neighbourhood_attention",
    )(proj, proj, proj, cache_k4, cache_v4, bias, q_norm_w.reshape(1, -1), k_norm_w.reshape(1, -1), mix)


def _tile_seq_pos(c, rt):
    n_p = N_PROMPT // rt
    pp, ps = SEQ // rt, DEC_SEQ // rt
    pos = jnp.where(c < n_p, c % pp, (c - n_p) % ps)
    per = jnp.where(c < n_p, pp, ps)
    return pos == 0, pos == per - 1


def _halo_specs(rt, width, col_of, halo):
    per = rt // halo
    last = N_TOK // halo - 1
    return [
        pl.BlockSpec((halo, width), lambda c, j: (jnp.maximum(c * per - 1, 0), col_of(j))),
        pl.BlockSpec((rt, width), lambda c, j: (c, col_of(j))),
        pl.BlockSpec((halo, width), lambda c, j: (jnp.minimum((c + 1) * per, last), col_of(j))),
    ]


def _ssd_conv_body(prev_ref, cur_ref, next_ref, w_ref, b_ref, o_ref, ext_ref, *, rt):
    first, last = _tile_seq_pos(pl.program_id(0), rt)
    ext_ref[0:HALO] = jnp.where(first, 0.0, prev_ref[...])
    ext_ref[HALO:HALO + rt] = cur_ref[...]
    ext_ref[HALO + rt:] = jnp.where(last, 0.0, next_ref[...])
    pad = SSD_CONV_W // 2
    ext = ext_ref[...]
    acc = b_ref[...]
    for k in range(SSD_CONV_W):
        shifted = ext if k == pad else pltpu.roll(ext, (pad - k) % ext.shape[0], 0)
        acc = acc + shifted[HALO:HALO + rt] * w_ref[k:k + 1]
    o_ref[...] = _silu(acc)


def ssd_conv(proj, conv_w, conv_b, rt=ROW_TILE, width=1024):
    col0 = (3 * NA_WIDTH + SSD_INNER) // width
    return pl.pallas_call(
        functools.partial(_ssd_conv_body, rt=rt),
        grid=(N_TOK // rt, SSD_CONV_DIM // width),
        in_specs=[
            *_halo_specs(rt, width, lambda j: col0 + j, HALO),
            pl.BlockSpec((SSD_CONV_W, width), lambda c, j: (0, j)),
            pl.BlockSpec((1, width), lambda c, j: (0, j)),
        ],
        out_specs=pl.BlockSpec((rt, width), lambda c, j: (c, j)),
        out_shape=jax.ShapeDtypeStruct((N_TOK, SSD_CONV_DIM), F32),
        scratch_shapes=[pltpu.VMEM((rt + 2 * HALO, width), F32)],
        compiler_params=_cparams(("parallel", "parallel")),
        name="ssd_conv",
    )(proj, proj, proj, conv_w, conv_b.reshape(1, -1))


def _short_conv_body(gb_ref, cp_ref, cc_ref, cn_ref, xp_ref, xc_ref, xn_ref, w_ref, o_ref, ext_ref, *, rt):
    first, last = _tile_seq_pos(pl.program_id(0), rt)

    def prod(c_ref, x_ref):
        return c_ref[...].astype(F32) * x_ref[...].astype(F32)

    h = HALO_BF16
    ext_ref[0:h] = jnp.where(first, 0.0, prod(cp_ref, xp_ref))
    ext_ref[h:h + rt] = prod(cc_ref, xc_ref)
    ext_ref[h + rt:] = jnp.where(last, 0.0, prod(cn_ref, xn_ref))
    pad = SC_CONV_W // 2
    ext = ext_ref[...]
    acc = None
    for k in range(SC_CONV_W):
        shifted = ext if k == pad else pltpu.roll(ext, (pad - k) % ext.shape[0], 0)
        term = shifted[h:h + rt] * w_ref[k:k + 1]
        acc = term if acc is None else acc + term
    o_ref[...] = (gb_ref[...].astype(F32) * acc).astype(o_ref.dtype)


def short_conv_gate(proj, conv_w, rt=ROW_TILE, width=1024):
    nb = SC_WIDTH // width
    return pl.pallas_call(
        functools.partial(_short_conv_body, rt=rt),
        grid=(N_TOK // rt, nb),
        in_specs=[
            pl.BlockSpec((rt, width), lambda c, j: (c, j)),
            *_halo_specs(rt, width, lambda j: nb + j, HALO_BF16),
            *_halo_specs(rt, width, lambda j: 2 * nb + j, HALO_BF16),
            pl.BlockSpec((SC_CONV_W, width), lambda c, j: (0, j)),
        ],
        out_specs=pl.BlockSpec((rt, width), lambda c, j: (c, j)),
        out_shape=jax.ShapeDtypeStruct((N_TOK, SC_WIDTH), BF16),
        scratch_shapes=[pltpu.VMEM((rt + 2 * HALO_BF16, width), F32)],
        compiler_params=_cparams(("parallel", "parallel")),
        name="short_conv_gate",
    )(proj, proj, proj, proj, proj, proj, proj, conv_w)


def _chunk_cumsum(x, reverse):
    rows = lax.broadcasted_iota(jnp.int32, x.shape, 0)
    n = x.shape[0]
    k = 1
    while k < n:
        if reverse:
            x = x + jnp.where(rows < n - k, pltpu.roll(x, n - k, 0), 0.0)
        else:
            x = x + jnp.where(rows >= k, pltpu.roll(x, k, 0), 0.0)
        k *= 2
    return x


def _dt_and_logdecay(dt_ref, bias_ref, a_ref):
    dtv = _softplus(dt_ref[...] + bias_ref[...])
    dta = dtv * a_ref[...]
    lane = lax.broadcasted_iota(jnp.int32, dta.shape, 1)
    acum = jnp.where(lane < SSD_HEADS, _chunk_cumsum(dta, False), _chunk_cumsum(dta, True))
    return dtv, acum


def _pair_cols(m, c0, lane):
    shape = (m.shape[0], LANES)
    return jnp.where(lane[:m.shape[0]] < SSD_HEAD_DIM,
                     jnp.broadcast_to(m[:, c0:c0 + 1], shape),
                     jnp.broadcast_to(m[:, c0 + 1:c0 + 2], shape))


def _ssd_state_body(xf_ref, bf_ref, dtf_ref, xb_ref, bb_ref, dtb_ref, h0_ref, bias_ref, a_ref, *rest, nc, final):
    h_ref = rest[-1]
    hpf_ref, hpb_ref = rest[-4:-2] if final else rest[-3:-1]
    t = pl.program_id(1)

    @pl.when(t == 0)
    def _():
        h_ref[...] = h0_ref[...]

    lane = lax.broadcasted_iota(jnp.int32, (SSD_CHUNK, LANES), 1)
    for d, (x_ref, b_ref, dt_ref, hp_ref) in enumerate(
            ((xf_ref, bf_ref, dtf_ref, hpf_ref), (xb_ref, bb_ref, dtb_ref, hpb_ref))):
        dtv, acum = _dt_and_logdecay(dt_ref, bias_ref, a_ref)
        edge = acum[SSD_CHUNK - 1:SSD_CHUNK] if d == 0 else acum[0:1]
        s = dtv * jnp.exp(edge - acum)
        cd = jnp.exp(edge)
        hp_ref[...] = h_ref[d].astype(BF16)
        for g in range(SSD_GROUPS):
            bt = jnp.transpose(b_ref[:, g * SSD_STATE:(g + 1) * SSD_STATE]).astype(BF16)
            for pr in range(SSD_HPG // 2):
                head = g * SSD_HPG + 2 * pr
                c0 = d * SSD_HEADS + head
                cols = slice(head * SSD_HEAD_DIM, head * SSD_HEAD_DIM + LANES)
                xd = (x_ref[:, cols] * _pair_cols(s, c0, lane)).astype(BF16)
                st = jnp.dot(bt, xd, preferred_element_type=F32)
                h_ref[d, :, cols] = h_ref[d, :, cols] * _pair_cols(cd, c0, lane) + st

    if final:
        hfin_ref = rest[-2]

        @pl.when(t == nc - 1)
        def _():
            for d in range(2):
                for c in range(0, SSD_INNER, LANES):
                    hfin_ref[d, c:c + LANES, :] = jnp.transpose(h_ref[d, :, c:c + LANES])


def ssd_states(xbc, dt_raw, h0t, dt_bias, a_neg, row0, nseq, nc, final=None):
    c0 = row0 // SSD_CHUNK
    hp = SSD_INNER
    bcol = SSD_INNER // (SSD_GROUPS * SSD_STATE)
    out_specs = [
        pl.BlockSpec((None, None, SSD_STATE, hp), lambda s, t: (s, t, 0, 0)),
        pl.BlockSpec((None, None, SSD_STATE, hp), lambda s, t: (s, nc - 1 - t, 0, 0)),
    ]
    out_shape = [jax.ShapeDtypeStruct((nseq, nc, SSD_STATE, hp), BF16)] * 2
    carried, aliases = (), {}
    if final is not None:
        j, n_even, earlier = final
        out_specs.append(pl.BlockSpec((None, None, 2, hp, SSD_STATE), lambda s, t: (s, j, 0, 0, 0)))
        out_shape.append(jax.ShapeDtypeStruct((nseq, n_even, 2, hp, SSD_STATE), F32))
        if earlier is not None:
            carried, aliases = (earlier,), {9: 2}

    def fwd(s, t):
        return c0 + s * nc + t

    def bwd(s, t):
        return c0 + s * nc + (nc - 1 - t)

    return pl.pallas_call(
        functools.partial(_ssd_state_body, nc=nc, final=final is not None),
        grid=(nseq, nc),
        in_specs=[
            pl.BlockSpec((SSD_CHUNK, hp), lambda s, t: (fwd(s, t), 0)),
            pl.BlockSpec((SSD_CHUNK, SSD_GROUPS * SSD_STATE), lambda s, t: (fwd(s, t), bcol)),
            pl.BlockSpec((SSD_CHUNK, LANES), lambda s, t: (fwd(s, t), 0)),
            pl.BlockSpec((SSD_CHUNK, hp), lambda s, t: (bwd(s, t), 0)),
            pl.BlockSpec((SSD_CHUNK, SSD_GROUPS * SSD_STATE), lambda s, t: (bwd(s, t), bcol)),
            pl.BlockSpec((SSD_CHUNK, LANES), lambda s, t: (bwd(s, t), 0)),
            pl.BlockSpec((None, 2, SSD_STATE, hp), lambda s, t: (s, 0, 0, 0)),
            pl.BlockSpec((1, LANES), lambda s, t: (0, 0)),
            pl.BlockSpec((1, LANES), lambda s, t: (0, 0)),
            *[pl.BlockSpec(memory_space=pl.ANY) for _ in carried],
        ],
        out_specs=out_specs,
        out_shape=out_shape,
        input_output_aliases=aliases,
        scratch_shapes=[pltpu.VMEM((2, SSD_STATE, hp), F32)],
        compiler_params=_cparams(("parallel", "arbitrary")),
        name="ssd_states",
    )(xbc, xbc, dt_raw, xbc, xbc, dt_raw, h0t, dt_bias, a_neg, *carried)


def _ssd_out_body(x_ref, b_ref, c_ref, z_ref, dt_ref, hpf_ref, hpb_ref, bias_ref, a_ref, dsk_ref, nw_ref,
                  mix_ref, o_ref, g_ref):
    del mix_ref
    dtv, acum = _dt_and_logdecay(dt_ref, bias_ref, a_ref)
    acum_t = jnp.transpose(acum)
    dt_t = jnp.transpose(dtv)
    shape = (SSD_CHUNK, SSD_CHUNK)
    rows = lax.broadcasted_iota(jnp.int32, shape, 0)
    lane = lax.broadcasted_iota(jnp.int32, shape, 1)
    lower = rows >= lane
    upper = rows <= lane

    def head_terms(cb, head):
        cf, cbk = head, SSD_HEADS + head
        col_f = jnp.broadcast_to(acum[:, cf:cf + 1], shape)
        col_b = jnp.broadcast_to(acum[:, cbk:cbk + 1], shape)
        lf = jnp.exp(jnp.where(lower, col_f - acum_t[cf:cf + 1, :], -jnp.inf)) * dt_t[cf:cf + 1, :]
        lb = jnp.exp(jnp.where(upper, col_b - acum_t[cbk:cbk + 1, :], -jnp.inf)) * dt_t[cbk:cbk + 1, :]
        return (cb * (lf + lb)).astype(BF16), jnp.exp(col_f), jnp.exp(col_b)

    for g in range(SSD_GROUPS):
        gcols = slice(g * SSD_GROUP_W, (g + 1) * SSD_GROUP_W)
        bg = b_ref[:, g * SSD_STATE:(g + 1) * SSD_STATE].astype(BF16)
        cg = c_ref[:, g * SSD_STATE:(g + 1) * SSD_STATE].astype(BF16)
        cb = _nt_dot(cg, bg)
        yoff_f = jnp.dot(cg, hpf_ref[:, gcols], preferred_element_type=F32)
        yoff_b = jnp.dot(cg, hpb_ref[:, gcols], preferred_element_type=F32)
        for pr in range(SSD_HPG // 2):
            head = g * SSD_HPG + 2 * pr
            cols = slice(head * SSD_HEAD_DIM, head * SSD_HEAD_DIM + LANES)
            pcols = slice(pr * LANES, (pr + 1) * LANES)
            xp = x_ref[:, cols]
            xb = xp.astype(BF16)
            first = lane < SSD_HEAD_DIM
            w0, ef0, eb0 = head_terms(cb, head)
            w1, ef1, eb1 = head_terms(cb, head + 1)
            y = jnp.where(first, jnp.dot(w0, xb, preferred_element_type=F32),
                          jnp.dot(w1, xb, preferred_element_type=F32))
            y = y + jnp.where(first, ef0, ef1) * yoff_f[:, pcols]
            y = y + jnp.where(first, eb0, eb1) * yoff_b[:, pcols]
            y = y + xp * dsk_ref[:, cols]
            g_ref[:, cols] = y * _silu(z_ref[:, cols])
        gg = g_ref[:, gcols]
        ms = jnp.mean(gg * gg, axis=-1, keepdims=True)
        o_ref[:, gcols] = (gg * lax.rsqrt(ms + EPS) * nw_ref[:, gcols]).astype(o_ref.dtype)


def ssd_outputs(xbc, proj, dt_raw, hpf, hpb, dt_bias, a_neg, d_skip, norm_w, row0, nseq, nc, mix):
    c0 = row0 // SSD_CHUNK
    hp = SSD_INNER
    gw = SSD_GROUPS * SSD_STATE
    bcol = SSD_INNER // gw
    zcol = 3 * NA_WIDTH // SSD_INNER

    def row(s, t):
        return c0 + s * nc + t

    return pl.pallas_call(
        _ssd_out_body,
        grid=(nseq, nc),
        in_specs=[
            pl.BlockSpec((SSD_CHUNK, hp), lambda s, t: (row(s, t), 0)),
            pl.BlockSpec((SSD_CHUNK, gw), lambda s, t: (row(s, t), bcol)),
            pl.BlockSpec((SSD_CHUNK, gw), lambda s, t: (row(s, t), bcol + 1)),
            pl.BlockSpec((SSD_CHUNK, hp), lambda s, t: (row(s, t), zcol)),
            pl.BlockSpec((SSD_CHUNK, LANES), lambda s, t: (row(s, t), 0)),
            pl.BlockSpec((None, None, SSD_STATE, hp), lambda s, t: (s, t, 0, 0)),
            pl.BlockSpec((None, None, SSD_STATE, hp), lambda s, t: (s, t, 0, 0)),
            pl.BlockSpec((1, LANES), lambda s, t: (0, 0)),
            pl.BlockSpec((1, LANES), lambda s, t: (0, 0)),
            pl.BlockSpec((1, hp), lambda s, t: (0, 0)),
            pl.BlockSpec((1, hp), lambda s, t: (0, 0)),
            pl.BlockSpec(memory_space=pl.ANY),
        ],
        out_specs=pl.BlockSpec((SSD_CHUNK, hp), lambda s, t: (row(s, t), NA_WIDTH // hp)),
        out_shape=jax.ShapeDtypeStruct(mix.shape, mix.dtype),
        input_output_aliases={11: 0},
        scratch_shapes=[pltpu.VMEM((SSD_CHUNK, hp), F32)],
        compiler_params=_cparams(("parallel", "parallel")),
        name="ssd_outputs",
    )(xbc, xbc, xbc, proj, dt_raw, hpf, hpb, dt_bias, a_neg, d_skip, norm_w, mix)


def _pad_lanes(v):
    return jnp.pad(v.reshape(1, -1), ((0, 0), (0, LANES - v.size)))


def ssd_mixer(proj, dt_raw, state_h0, conv_w, conv_b, a_log, dt_bias, d_skip, norm_w, mix, final):
    xbc = ssd_conv(proj, conv_w, conv_b)
    bias = _pad_lanes(dt_bias)
    a_neg = _pad_lanes(-jnp.exp(a_log))
    dsk = jnp.repeat(d_skip, SSD_HEAD_DIM).reshape(1, -1)
    nw = norm_w.reshape(1, -1)

    def to_t(h):
        return jnp.transpose(h, (0, 1, 4, 2, 3)).reshape(h.shape[0], 2, SSD_STATE, SSD_INNER)

    nc_p, nc_s = SEQ // SSD_CHUNK, DEC_SEQ // SSD_CHUNK
    h0_p = jnp.zeros((BATCH, 2, SSD_STATE, SSD_INNER), F32)
    hpf, hpb, hfin = ssd_states(xbc, dt_raw, h0_p, bias, a_neg, 0, BATCH, nc_p, final)
    mix = ssd_outputs(xbc, proj, dt_raw, hpf, hpb, bias, a_neg, dsk, nw, 0, BATCH, nc_p, mix)
    hpf, hpb = ssd_states(xbc, dt_raw, to_t(state_h0), bias, a_neg, N_PROMPT, DEC_BATCH, nc_s)
    mix = ssd_outputs(xbc, proj, dt_raw, hpf, hpb, bias, a_neg, dsk, nw, N_PROMPT, DEC_BATCH, nc_s, mix)
    return mix, hfin


def kernel(x_prompt, x_sample, cache_k, cache_v, state_ssm, c, c_ctx, norm1_w, norm2_w, w_mod, b_mod,
           w_in_even, w_out_even, q_norm_w, k_norm_w, na_rel_bias, ssd_conv_w, ssd_conv_b, ssd_a_log,
           ssd_dt_bias, ssd_d, ssd_norm_w, w_in_odd, sc_conv_w, w_out_odd, w_ffn_gate, w_ffn_up,
           w_ffn_down):
    d = D_MODEL
    x = (x_prompt.reshape(N_PROMPT, d), x_sample.reshape(N_SAMPLE, d))
    cond8 = jnp.concatenate([c_ctx[None, :], c, jnp.zeros((8 - 1 - DEC_BATCH, d), F32)], axis=0)
    mods = adaln_table(cond8, w_mod, b_mod)
    n_even = w_in_even.shape[0]
    cache_k4 = cache_k.reshape(DEC_BATCH, n_even, PAST_LEN, NA_WIDTH)
    cache_v4 = cache_v.reshape(DEC_BATCH, n_even, PAST_LEN, NA_WIDTH)

    w_down = cast_weights_bf16(w_ffn_down)
    caches, states = None, None
    for l in range(DEPTH):
        j = l // 2
        h = modnorm(x, norm1_w[l], mods, l, 0)
        if l % 2 == 0:
            w_dt = jnp.pad(w_in_even[j:j + 1, :, EVEN_MAIN:], ((0, 0), (0, 0), (0, LANES - 2 * SSD_HEADS)))
            pr = proj(h, w_in_even, j, EVEN_MAIN, F32)
            dt_raw = proj(h, w_dt, 0, LANES, F32, tn=LANES)
            mix, *caches = dense_attention(pr, q_norm_w[j], k_norm_w[j], j, n_even, caches)
            mix = neighbourhood_attention(pr, cache_k4, cache_v4, j, na_rel_bias[j],
                                          q_norm_w[j], k_norm_w[j], mix)
            mix, states = ssd_mixer(pr, dt_raw, state_ssm[:, j], ssd_conv_w[j], ssd_conv_b[j],
                                    ssd_a_log[j].reshape(-1), ssd_dt_bias[j].reshape(-1), ssd_d[j],
                                    ssd_norm_w[j], mix, (j, n_even, states))
            x_mid = mixer_out_proj(mix, w_out_even, j, x, mods, l)
        else:
            pr = proj(h, w_in_odd, j, 3 * SC_WIDTH, BF16)
            mix = short_conv_gate(pr, sc_conv_w[j])
            x_mid = mixer_out_proj(mix, w_out_odd, j, x, mods, l)
        h = modnorm((x_mid,), norm2_w[l], mods, l, 3)
        hidden = swiglu(h, w_ffn_gate, w_ffn_up, l)
        if l < DEPTH - 1:
            x = (ffn_down_proj(hidden, w_down, x_mid, mods, l),)
        else:
            y_prompt = ffn_down_proj(hidden, w_down, x_mid, mods, l, 0, N_PROMPT).reshape(BATCH, SEQ, d)
            y_sample = ffn_down_proj(hidden, w_down, x_mid, mods, l, N_PROMPT, N_SAMPLE).reshape(DEC_BATCH, DEC_SEQ, d)

    new_k, new_v = (t.reshape(BATCH, n_even, SEQ, NA_HEADS, NA_HEAD_DIM) for t in caches)
    new_h = states.reshape(BATCH, n_even, 2, SSD_HEADS, SSD_HEAD_DIM, SSD_STATE)
    return (y_prompt, y_sample, new_k, new_v, new_h)
```
